```python
import math
import jax, jax.numpy as jnp
from jax import lax
import numpy as np

D_MODEL = 1024
BATCH = 4
SEQ = 4096
DEPTH = 4

GRID_W = 64
CTX_LEN = 256

N_MIXERS = 4
MIX_CONV, MIX_NAT, MIX_MLA, MIX_FNET = 0, 1, 2, 3
N_MOD = 6

FFN_HIDDEN = -(-8 * D_MODEL // (3 * 256)) * 256

NA_HEADS = 16
NA_HEAD_DIM = D_MODEL // NA_HEADS
NA_WIN_ROWS = 8
NA_WIN_COLS = 16

MLA_HEADS = 16
MLA_Q_RANK = D_MODEL // 4
MLA_KV_RANK = D_MODEL // 4
MLA_NOPE = 64
MLA_ROPE = 32
MLA_V = 64

FNET_GROUPS = 4

ROPE_BASE = 10000.0
NORM_EPS = 1e-6
Q_BLOCK = 128

kernel_name = "hybrid_interleaved_diffusion_trunk"


def rmsnorm(x, g):
    x32 = x.astype(jnp.float32)
    y = x32 * lax.rsqrt(jnp.mean(x32 * x32, axis=-1, keepdims=True) + NORM_EPS)
    return y.astype(x.dtype) * g


def modulate(h, shift, scale):
    return h * (1 + scale) + shift


def axial_rope_tables(n_tokens, rot_dim):
    n_freq = rot_dim // 4
    freq = ROPE_BASE ** (-jnp.arange(n_freq, dtype=jnp.float32) / n_freq)
    t = jnp.arange(n_tokens)
    row = (t // GRID_W).astype(jnp.float32)
    col = (t % GRID_W).astype(jnp.float32)
    ang = jnp.concatenate([row[:, None] * freq, col[:, None] * freq], axis=-1)
    return jnp.cos(ang), jnp.sin(ang)


def apply_rope(x, cos, sin):
    half = x.shape[-1] // 2
    x1, x2 = x[..., :half], x[..., half:]
    return jnp.concatenate([x1 * cos - x2 * sin, x1 * sin + x2 * cos], axis=-1).astype(x.dtype)


def blocked_attention(q, k, v, scale):
    B, Sq, H, dq = q.shape
    nb = Sq // Q_BLOCK
    qb = q.reshape(B, nb, Q_BLOCK, H, dq).transpose(1, 0, 2, 3, 4)

    def one_block(q_blk):
        s = jnp.einsum('bqhd,bkhd->bhqk', q_blk, k).astype(jnp.float32) * scale
        p = jax.nn.softmax(s, axis=-1).astype(v.dtype)
        return jnp.einsum('bhqk,bkhd->bqhd', p, v)

    o = lax.map(one_block, qb)
    return o.transpose(1, 0, 2, 3, 4).reshape(B, Sq, H, v.shape[-1])


def swiglu(h, w_in, w_out):
    gate, up = jnp.split(h @ w_in, 2, axis=-1)
    return (jax.nn.silu(gate) * up) @ w_out


def short_conv_mixer(h, w_in, conv_w, w_out):
    b_gate, c_gate, v = jnp.split(h @ w_in, 3, axis=-1)
    u = c_gate * v
    up = jnp.pad(u, ((0, 0), (1, 1), (0, 0)))
    z = conv_w[0] * up[:, :-2] + conv_w[1] * up[:, 1:-1] + conv_w[2] * up[:, 2:]
    return (b_gate * z) @ w_out


def neighbourhood_attention(hl, hc, w_qkv, rpb, w_o, want_ctx):
    B, S, D = hl.shape
    H, dh = NA_HEADS, NA_HEAD_DIM
    rows = S // GRID_W
    kr = min(NA_WIN_ROWS, rows)
    kc = NA_WIN_COLS
    scale = dh ** -0.5

    qkv = hl @ w_qkv
    to_grid = lambda t: t.reshape(B, rows, GRID_W, H, dh).transpose(0, 3, 1, 2, 4)
    qg, kg, vg = to_grid(qkv[..., :D]), to_grid(qkv[..., D:2 * D]), to_grid(qkv[..., 2 * D:])

    kv_c = hc @ w_qkv[:, D:]
    k_ctx = kv_c[..., :D].reshape(B, -1, H, dh)
    v_ctx = kv_c[..., D:].reshape(B, -1, H, dh)

    cols = jnp.arange(GRID_W)
    c0 = jnp.clip(cols - kc // 2, 0, GRID_W - kc)
    col_idx = c0[:, None] + jnp.arange(kc)
    col_bias = rpb[:, :, col_idx - cols[:, None] + NA_WIN_COLS - 1]

    def row_block(r):
        r0 = jnp.clip(r - kr // 2, 0, rows - kr)
        q_r = lax.dynamic_index_in_dim(qg, r, axis=2, keepdims=False)
        k_band = lax.dynamic_slice_in_dim(kg, r0, kr, axis=2)
        v_band = lax.dynamic_slice_in_dim(vg, r0, kr, axis=2)
        k_win = k_band[:, :, :, col_idx]
        v_win = v_band[:, :, :, col_idx]
        row_off = r0 + jnp.arange(kr) - r + NA_WIN_ROWS - 1
        bias = jnp.take(col_bias, row_off, axis=1).transpose(0, 2, 1, 3)
        s_win = (jnp.einsum('bhqd,bhrqcd->bhqrc', q_r, k_win).astype(jnp.float32) * scale
                 + bias[None].astype(jnp.float32))
        s_ctx = jnp.einsum('bhqd,bkhd->bhqk', q_r, k_ctx).astype(jnp.float32) * scale
        s = jnp.concatenate([s_win.reshape(B, H, GRID_W, kr * kc), s_ctx], axis=-1)
        p = jax.nn.softmax(s, axis=-1).astype(v_win.dtype)
        p_win = p[..., :kr * kc].reshape(B, H, GRID_W, kr, kc)
        p_ctx = p[..., kr * kc:]
        return (jnp.einsum('bhqrc,bhrqcd->bhqd', p_win, v_win)
                + jnp.einsum('bhqk,bkhd->bhqd', p_ctx, v_ctx))

    o = lax.map(row_block, jnp.arange(rows))
    yl = o.transpose(1, 0, 3, 2, 4).reshape(B, S, D) @ w_o

    yc = None
    if want_ctx:
        q_ctx = (hc @ w_qkv[:, :D]).reshape(B, -1, H, dh)
        yc = blocked_attention(q_ctx, k_ctx, v_ctx, scale).reshape(B, -1, D) @ w_o
    return yl, yc


def mla_queries(h, w_dq, q_g, w_uq):
    B, S, _ = h.shape
    q = (rmsnorm(h @ w_dq, q_g) @ w_uq).reshape(B, S, MLA_HEADS, MLA_NOPE + MLA_ROPE)
    return q[..., :MLA_NOPE], q[..., MLA_NOPE:]


def mla_keys_values(h, w_dkv, kv_g, w_ukv):
    B, S, _ = h.shape
    ckv = h @ w_dkv
    c_kv = rmsnorm(ckv[..., :MLA_KV_RANK], kv_g)
    k_rope = ckv[..., MLA_KV_RANK:]
    kv = (c_kv @ w_ukv).reshape(B, S, MLA_HEADS, MLA_NOPE + MLA_V)
    return kv[..., :MLA_NOPE], k_rope, kv[..., MLA_NOPE:]


def mla_join(nope, rope):
    if rope.ndim == 3:
        rope = jnp.broadcast_to(rope[:, :, None, :], nope.shape[:3] + (rope.shape[-1],))
    return jnp.concatenate([nope, rope], axis=-1)


def latent_attention(hl, hc, w_dq, q_g, w_uq, w_dkv, kv_g, w_ukv, w_o, cos, sin, want_ctx):
    B, S, _ = hl.shape
    scale = (MLA_NOPE + MLA_ROPE) ** -0.5
    qn, qr = mla_queries(hl, w_dq, q_g, w_uq)
    kn, kr, v = mla_keys_values(hl, w_dkv, kv_g, w_ukv)
    q_l = mla_join(qn, apply_rope(qr, cos[:, None, :], sin[:, None, :]))
    k_l = mla_join(kn, apply_rope(kr, cos, sin))
    kn_c, kr_c, v_c = mla_keys_values(hc, w_dkv, kv_g, w_ukv)
    k_c = mla_join(kn_c, kr_c)
    k_all = jnp.concatenate([k_l, k_c], axis=1)
    v_all = jnp.concatenate([v, v_c], axis=1)
    yl = blocked_attention(q_l, k_all, v_all, scale).reshape(B, S, MLA_HEADS * MLA_V) @ w_o
    yc = None
    if want_ctx:
        qn_c, qr_c = mla_queries(hc, w_dq, q_g, w_uq)
        o_c = blocked_attention(mla_join(qn_c, qr_c), k_c, v_c, scale)
        yc = o_c.reshape(B, -1, MLA_HEADS * MLA_V) @ w_o
    return yl, yc


def fourier_mixer(h, w_o):
    B, S, D = h.shape
    hg = h.astype(jnp.float32).reshape(B, S, FNET_GROUPS, D // FNET_GROUPS)
    f = jnp.fft.fftn(hg, axes=(1, 3), norm="ortho").real
    return f.reshape(B, S, D).astype(h.dtype) @ w_o


def setup_inputs(seed: int = 0) -> dict:
    key = jax.random.key(seed)
    ks = iter(jax.random.split(key, 32))
    D, F = D_MODEL, FFN_HIDDEN
    n_conv = len(range(MIX_CONV, DEPTH, N_MIXERS))
    n_nat = len(range(MIX_NAT, DEPTH, N_MIXERS))
    n_mla = len(range(MIX_MLA, DEPTH, N_MIXERS))
    n_fnet = len(range(MIX_FNET, DEPTH, N_MIXERS))

    def nrm(shape, scale):
        return jax.random.normal(next(ks), shape, jnp.float32) * scale

    def gain(shape):
        return 1.0 + nrm(shape, 0.05)

    return {
        "x": nrm((BATCH, SEQ, D), 1.0),
        "c": nrm((BATCH, D), 1.0),
        "ctx": nrm((BATCH, CTX_LEN, D), 1.0),
        "c_ctx": nrm((D,), 1.0),
        "mod_w": nrm((DEPTH, D, N_MOD * D), 0.5 * D ** -0.5),
        "mod_b": nrm((DEPTH, N_MOD * D), 0.02),
        "mix_norm_g": gain((DEPTH, D)),
        "ffn_norm_g": gain((DEPTH, D)),
        "conv_w_in": nrm((n_conv, D, 3 * D), D ** -0.5),
        "conv_w": nrm((n_conv, 3, D), 3 ** -0.5),
        "conv_w_out": nrm((n_conv, D, D), D ** -0.5),
        "nat_w_qkv": nrm((n_nat, D, 3 * D), D ** -0.5),
        "nat_rpb": nrm((n_nat, NA_HEADS, 2 * NA_WIN_ROWS - 1, 2 * NA_WIN_COLS - 1), 0.1),
        "nat_w_o": nrm((n_nat, D, D), D ** -0.5),
        "mla_w_dq": nrm((n_mla, D, MLA_Q_RANK), D ** -0.5),
        "mla_q_norm_g": gain((n_mla, MLA_Q_RANK)),
        "mla_w_uq": nrm((n_mla, MLA_Q_RANK, MLA_HEADS * (MLA_NOPE + MLA_ROPE)), MLA_Q_RANK ** -0.5),
        "mla_w_dkv": nrm((n_mla, D, MLA_KV_RANK + MLA_ROPE), D ** -0.5),
        "mla_kv_norm_g": gain((n_mla, MLA_KV_RANK)),
        "mla_w_ukv": nrm((n_mla, MLA_KV_RANK, MLA_HEADS * (MLA_NOPE + MLA_V)), MLA_KV_RANK ** -0.5),
        "mla_w_o": nrm((n_mla, MLA_HEADS * MLA_V, D), (MLA_HEADS * MLA_V) ** -0.5),
        "fnet_w_o": nrm((n_fnet, D, D), D ** -0.5),
        "ffn_w_in": nrm((DEPTH, D, 2 * F), D ** -0.5),
        "ffn_w_out": nrm((DEPTH, F, D), F ** -0.5),
        "final_norm_g": gain((D,)),
    }


def reference(x, c, ctx, c_ctx, mod_w, mod_b, mix_norm_g, ffn_norm_g, conv_w_in, conv_w,
              conv_w_out, nat_w_qkv, nat_rpb, nat_w_o, mla_w_dq, mla_q_norm_g, mla_w_uq,
              mla_w_dkv, mla_kv_norm_g, mla_w_ukv, mla_w_o, fnet_w_o, ffn_w_in, ffn_w_out,
              final_norm_g):
    S = x.shape[1]
    rope_cos, rope_sin = axial_rope_tables(S, MLA_ROPE)
    silu_c = jax.nn.silu(c)
    silu_cc = jax.nn.silu(c_ctx)
    xl, xc = x, ctx
    for i in range(DEPTH):
        kind, j = i % N_MIXERS, i // N_MIXERS
        reads_ctx = kind in (MIX_NAT, MIX_MLA)
        ctx_needed = any((l % N_MIXERS) in (MIX_NAT, MIX_MLA) for l in range(i + 1, DEPTH))

        mod_l = jnp.split(silu_c @ mod_w[i] + mod_b[i], N_MOD, axis=-1)
        sh1, sc1, g1, sh2, sc2, g2 = [m[:, None, :] for m in mod_l]
        hl = modulate(rmsnorm(xl, mix_norm_g[i]), sh1, sc1)
        hc = None
        if reads_ctx or ctx_needed:
            mc = jnp.split(silu_cc @ mod_w[i] + mod_b[i], N_MOD, axis=-1)
            hc = modulate(rmsnorm(xc, mix_norm_g[i]), mc[0], mc[1])

        if kind == MIX_CONV:
            yl = short_conv_mixer(hl, conv_w_in[j], conv_w[j], conv_w_out[j])
            yc = short_conv_mixer(hc, conv_w_in[j], conv_w[j], conv_w_out[j]) if ctx_needed else None
        elif kind == MIX_NAT:
            yl, yc = neighbourhood_attention(hl, hc, nat_w_qkv[j], nat_rpb[j], nat_w_o[j], ctx_needed)
        elif kind == MIX_MLA:
            yl, yc = latent_attention(hl, hc, mla_w_dq[j], mla_q_norm_g[j], mla_w_uq[j], mla_w_dkv[j],
                                      mla_kv_norm_g[j], mla_w_ukv[j], mla_w_o[j], rope_cos, rope_sin,
                                      ctx_needed)
        else:
            yl = fourier_mixer(hl, fnet_w_o[j])
            yc = fourier_mixer(hc, fnet_w_o[j]) if ctx_needed else None

        xl = xl + g1 * yl
        xl = xl + g2 * swiglu(modulate(rmsnorm(xl, ffn_norm_g[i]), sh2, sc2), ffn_w_in[i], ffn_w_out[i])
        if ctx_needed:
            xc = xc + mc[2] * yc
            xc = xc + mc[5] * swiglu(modulate(rmsnorm(xc, ffn_norm_g[i]), mc[3], mc[4]),
                                     ffn_w_in[i], ffn_w_out[i])
    return rmsnorm(xl, final_norm_g)
```

```python
import functools
import math

import numpy as np
import jax
import jax.numpy as jnp
from jax import lax
from jax.experimental import pallas as pl
from jax.experimental.pallas import tpu as pltpu

D = 1024
B = 4
S = 4096
CTX = 256
DEPTH = 4
GRID_W = 64
N_MOD = 6
F = 2816
EPS = 1e-6

T_LAT = B * S
T_CTX = B * CTX
NT = T_LAT + T_CTX

NA_H = 16
NA_KR = 8
NA_KC = 16
ML_H = 16
ML_RANK = 256
ML_NOPE = 64
ML_ROPE = 32
ML_V = 64
HP = 8
LANE = 128
MASK_NEG = -1e30

TM = 512
N_TILES = NT // TM
N_LAT_TILES = T_LAT // TM
TILES_PER_SEQ = S // TM
FFN_CK = 256
VMEM_LIMIT = 56 * 1024 * 1024

f32 = jnp.float32
bf16 = jnp.bfloat16


def _resident(shape):
    nd = len(shape)
    return pl.BlockSpec(shape, lambda *_: (0,) * nd, pipeline_mode=pl.Buffered(1))


def _mod_spec(layer):
    return pl.BlockSpec((None, 1, N_MOD * D), lambda t: (layer * 8 + t // TILES_PER_SEQ, 0, 0))


def _rmsnorm(x, g):
    ms = jnp.mean(x * x, axis=-1, keepdims=True)
    return (x * lax.rsqrt(ms + EPS)) * g


def _norm_mod(x, g, shift, scale):
    return _rmsnorm(x, g) * (1.0 + scale) + shift


def _dot(a, b):
    return jnp.dot(a, b, preferred_element_type=f32)


def _dot_nt(a, b):
    return lax.dot_general(a, b, (((1,), (1,)), ((), ())), preferred_element_type=f32)


def _mod_kernel(cc_ref, w_ref, b_ref, out_ref):
    cc = cc_ref[...]
    s = (cc * jax.nn.sigmoid(cc)).astype(bf16)
    out_ref[...] = _dot(s, w_ref[...].astype(bf16)) + b_ref[...]


def _mod_call(cc8, mod_w, mod_b):
    tn = 1536
    return pl.pallas_call(
        _mod_kernel,
        grid=(DEPTH, N_MOD * D // tn),
        in_specs=[
            pl.BlockSpec((8, D), lambda l, j: (0, 0)),
            pl.BlockSpec((None, D, tn), lambda l, j: (l, 0, j)),
            pl.BlockSpec((None, 1, tn), lambda l, j: (l, 0, j)),
        ],
        out_specs=pl.BlockSpec((None, 8, tn), lambda l, j: (l, 0, j)),
        out_shape=jax.ShapeDtypeStruct((DEPTH, 8, N_MOD * D), f32),
        compiler_params=pltpu.CompilerParams(vmem_limit_bytes=VMEM_LIMIT),
        name="mod_vectors",
    )(cc8, mod_w, mod_b.reshape(DEPTH, 1, N_MOD * D))


def _tail_kernel(x_ref, o_ref, mod_ref, wo_ref, g_ref, win_ref, wout_ref, *rest, final):
    if final:
        fg_ref, out_ref, acc_ref = rest
    else:
        out_ref, acc_ref = rest
    mod = mod_ref[...]
    g1 = mod[:, 2 * D:3 * D]
    sh2 = mod[:, 3 * D:4 * D]
    sc2 = mod[:, 4 * D:5 * D]
    g2 = mod[:, 5 * D:6 * D]
    x1 = x_ref[...] + g1 * _dot(o_ref[...], wo_ref[...])
    h = _norm_mod(x1, g_ref[...], sh2, sc2).astype(bf16)
    for c in range(F // FFN_CK):
        lo = c * FFN_CK
        gate = _dot(h, win_ref[:, lo:lo + FFN_CK])
        up = _dot(h, win_ref[:, F + lo:F + lo + FFN_CK])
        a = (gate * jax.nn.sigmoid(gate) * up).astype(bf16)
        y = _dot(a, wout_ref[lo:lo + FFN_CK, :])
        if c == 0:
            acc_ref[...] = y
        else:
            acc_ref[...] += y
    x2 = x1 + g2 * acc_ref[...]
    if final:
        x2 = _rmsnorm(x2, fg_ref[...])
    out_ref[...] = x2


def _tail_call(x, o, modv, layer, wo, g, win, wout, n_tiles, final_g=None):
    final = final_g is not None
    row = lambda w: pl.BlockSpec((TM, w), lambda t: (t, 0))
    in_specs = [row(D), row(D), _mod_spec(layer), _resident((D, D)), _resident((1, D)),
                _resident((D, 2 * F)), _resident((F, D))]
    args = [x, o, modv, wo, g.reshape(1, D), win, wout]
    if final:
        in_specs.append(_resident((1, D)))
        args.append(final_g.reshape(1, D))
    return pl.pallas_call(
        functools.partial(_tail_kernel, final=final),
        grid=(n_tiles,),
        in_specs=in_specs,
        out_specs=row(D),
        out_shape=jax.ShapeDtypeStruct((n_tiles * TM, D), f32),
        scratch_shapes=[pltpu.VMEM((TM, D), f32)],
        compiler_params=pltpu.CompilerParams(vmem_limit_bytes=VMEM_LIMIT),
        name=f"tail_l{layer}",
    )(*args)


CONV_HALO = 16
CONV_CK = 256


def _conv_kernel(xp_ref, x_ref, xn_ref, mod_ref, g_ref, win_ref, cw_ref, out_ref):
    t = pl.program_id(0)
    mod = mod_ref[...]
    sh1 = mod[:, 0:D]
    sc1 = mod[:, D:2 * D]
    g = g_ref[...]
    h = jnp.concatenate(
        [_norm_mod(r[...], g, sh1, sc1).astype(bf16) for r in (xp_ref, x_ref, xn_ref)], axis=0)
    rows = TM + 2 * CONV_HALO
    seq_len = jnp.where(t < N_LAT_TILES, S, CTX)
    grow = t * TM - CONV_HALO + lax.broadcasted_iota(jnp.int32, (rows, CONV_CK), 0)
    pos = jnp.bitwise_and(grow, seq_len - 1)
    has_prev = pos != 0
    has_next = pos != seq_len - 1
    for j in range(D // CONV_CK):
        lo = j * CONV_CK
        bg = _dot(h, win_ref[:, lo:lo + CONV_CK])
        cg = _dot(h, win_ref[:, D + lo:D + lo + CONV_CK])
        v = _dot(h, win_ref[:, 2 * D + lo:2 * D + lo + CONV_CK])
        u = cg * v
        u_prev = jnp.where(has_prev, pltpu.roll(u, 1, axis=0), 0.0)
        u_next = jnp.where(has_next, pltpu.roll(u, rows - 1, axis=0), 0.0)
        z = (cw_ref[0:1, lo:lo + CONV_CK] * u_prev + cw_ref[1:2, lo:lo + CONV_CK] * u
             + cw_ref[2:3, lo:lo + CONV_CK] * u_next)
        bz = bg * z
        out_ref[:, lo:lo + CONV_CK] = bz[CONV_HALO:CONV_HALO + TM].astype(bf16)


def _conv_call(x, modv, layer, g, win, cw):
    hb = TM // CONV_HALO
    last = NT // CONV_HALO - 1
    return pl.pallas_call(
        _conv_kernel,
        grid=(N_TILES,),
        in_specs=[
            pl.BlockSpec((CONV_HALO, D), lambda t: (jnp.maximum(t * hb - 1, 0), 0)),
            pl.BlockSpec((TM, D), lambda t: (t, 0)),
            pl.BlockSpec((CONV_HALO, D), lambda t: (jnp.minimum((t + 1) * hb, last), 0)),
            _mod_spec(layer), _resident((1, D)), _resident((D, 3 * D)), _resident((8, D)),
        ],
        out_specs=pl.BlockSpec((TM, D), lambda t: (t, 0)),
        out_shape=jax.ShapeDtypeStruct((NT, D), bf16),
        compiler_params=pltpu.CompilerParams(vmem_limit_bytes=VMEM_LIMIT),
        name="conv_mixer",
    )(x, x, x, modv, g.reshape(1, D), win, cw)


QKV_CK = 512
NA_SCALE = 0.125


def _qkv_kernel(x_ref, mod_ref, g_ref, w_ref, out_ref):
    mod = mod_ref[...]
    h = _norm_mod(x_ref[...], g_ref[...], mod[:, 0:D], mod[:, D:2 * D]).astype(bf16)
    for c in range(3 * D // QKV_CK):
        lo = c * QKV_CK
        r = _dot(h, w_ref[:, lo:lo + QKV_CK])
        if lo < D:
            r = r * NA_SCALE
        out_ref[:, lo:lo + QKV_CK] = r.astype(bf16)


def _qkv_call(x, modv, layer, g, w):
    return pl.pallas_call(
        _qkv_kernel,
        grid=(N_TILES,),
        in_specs=[pl.BlockSpec((TM, D), lambda t: (t, 0)), _mod_spec(layer),
                  _resident((1, D)), _resident((D, 3 * D))],
        out_specs=pl.BlockSpec((TM, 3 * D), lambda t: (t, 0)),
        out_shape=jax.ShapeDtypeStruct((NT, 3 * D), bf16),
        compiler_params=pltpu.CompilerParams(vmem_limit_bytes=VMEM_LIMIT),
        name="nat_qkv",
    )(x, modv, g.reshape(1, D), w)


def _stack_pair(q2):
    lane = lax.broadcasted_iota(jnp.int32, q2.shape, 1)
    zero = jnp.zeros_like(q2)
    return jnp.concatenate([jnp.where(lane < 64, q2, zero), jnp.where(lane >= 64, q2, zero)], axis=0)


def _unstack_pair(o, n):
    lane = lax.broadcasted_iota(jnp.int32, (n, LANE), 1)
    return jnp.where(lane < 64, o[:n], o[n:])


def _nat_kernel(q_ref, kb_ref, vb_ref, kc_ref, vc_ref, bias_ref, out_ref):
    r = pl.program_id(1)
    rows = S // GRID_W
    r0 = jnp.clip(r - NA_KR // 2, 0, rows - NA_KR)
    off = pl.multiple_of(r0 * GRID_W, GRID_W)
    nband = NA_KR * GRID_W
    for hp in range(HP):
        ls = slice(hp * LANE, (hp + 1) * LANE)
        qst = _stack_pair(q_ref[:, ls])
        s_w = _dot_nt(qst, kb_ref[pl.ds(off, nband), ls]) + bias_ref[hp]
        s_c = _dot_nt(qst, kc_ref[:, ls])
        m = jnp.maximum(jnp.max(s_w, axis=-1, keepdims=True), jnp.max(s_c, axis=-1, keepdims=True))
        p_w = jnp.exp(s_w - m)
        p_c = jnp.exp(s_c - m)
        l = jnp.sum(p_w, axis=-1, keepdims=True) + jnp.sum(p_c, axis=-1, keepdims=True)
        o = _dot(p_w.astype(bf16), vb_ref[pl.ds(off, nband), ls]) + _dot(p_c.astype(bf16), vc_ref[:, ls])
        out_ref[:, ls] = _unstack_pair(o / l, GRID_W).astype(bf16)


def _nat_call(qkv, bias):
    rows = S // GRID_W

    def bias_idx(b, r):
        r0 = jnp.clip(r - NA_KR // 2, 0, rows - NA_KR)
        return (r0 - r + NA_KR - 1, 0, 0, 0)

    ctx_blk = T_LAT // CTX
    return pl.pallas_call(
        _nat_kernel,
        grid=(B, rows),
        in_specs=[
            pl.BlockSpec((GRID_W, D), lambda b, r: (b * rows + r, 0)),
            pl.BlockSpec((S, D), lambda b, r: (b, 1)),
            pl.BlockSpec((S, D), lambda b, r: (b, 2)),
            pl.BlockSpec((CTX, D), lambda b, r: (ctx_blk + b, 1)),
            pl.BlockSpec((CTX, D), lambda b, r: (ctx_blk + b, 2)),
            pl.BlockSpec((None, HP, 2 * GRID_W, NA_KR * GRID_W), bias_idx),
        ],
        out_specs=pl.BlockSpec((GRID_W, D), lambda b, r: (b * rows + r, 0)),
        out_shape=jax.ShapeDtypeStruct((NT, D), bf16),
        compiler_params=pltpu.CompilerParams(vmem_limit_bytes=VMEM_LIMIT),
        name="nat_attn",
    )(qkv, qkv, qkv, qkv, qkv, bias)


def _ctx_attn_kernel(q_ref, k_ref, v_ref, o_in_ref, out_ref):
    del o_in_ref
    for hp in range(HP):
        ls = slice(hp * LANE, (hp + 1) * LANE)
        qst = _stack_pair(q_ref[:, ls])
        s = _dot_nt(qst, k_ref[:, ls])
        m = jnp.max(s, axis=-1, keepdims=True)
        p = jnp.exp(s - m)
        l = jnp.sum(p, axis=-1, keepdims=True)
        o = _dot(p.astype(bf16), v_ref[:, ls])
        out_ref[:, ls] = _unstack_pair(o / l, CTX).astype(bf16)


def _ctx_attn_call(qkv, o):
    ctx_blk = T_LAT // CTX
    return pl.pallas_call(
        _ctx_attn_kernel,
        grid=(B,),
        in_specs=[
            pl.BlockSpec((CTX, D), lambda b: (ctx_blk + b, 0)),
            pl.BlockSpec((CTX, D), lambda b: (ctx_blk + b, 1)),
            pl.BlockSpec((CTX, D), lambda b: (ctx_blk + b, 2)),
            pl.BlockSpec(memory_space=pl.ANY),
        ],
        out_specs=pl.BlockSpec((CTX, D), lambda b: (ctx_blk + b, 0)),
        out_shape=jax.ShapeDtypeStruct((NT, D), bf16),
        input_output_aliases={3: 0},
        compiler_params=pltpu.CompilerParams(vmem_limit_bytes=VMEM_LIMIT),
        name="nat_ctx_attn",
    )(qkv, qkv, qkv, o)


def _nat_bias_table(rpb):
    cols = np.arange(GRID_W)
    c0 = np.clip(cols - NA_KC // 2, 0, GRID_W - NA_KC)
    j = cols[None, :]
    q = cols[:, None]
    valid = (j >= c0[:, None]) & (j < c0[:, None] + NA_KC)
    rel = np.clip(j - q + NA_KC - 1, 0, 2 * NA_KC - 2)
    dense = jnp.where(valid[None, None], rpb[:, :, rel], MASK_NEG)
    slabs = []
    for d0 in range(NA_KR):
        band = dense[:, d0:d0 + NA_KR]
        slabs.append(band.transpose(0, 2, 1, 3).reshape(NA_H, GRID_W, NA_KR * GRID_W))
    tab = jnp.stack(slabs, axis=0)
    return tab.reshape(NA_KR, HP, 2 * GRID_W, NA_KR * GRID_W).astype(f32)


ML_SCALE = (ML_NOPE + ML_ROPE) ** -0.5
ML_QW = ML_H * LANE
ML_P1 = 2 * ML_RANK + 2 * LANE
ML_CK = 256


def _mla_proj_kernel(x_ref, mod_ref, g_ref, w1_ref, qg_ref, kvg_ref, wqa_ref, wqb_ref, wk_ref, wv_ref,
                     cos_ref, sin_ref, q_out, k_out, v_out):
    mod = mod_ref[...]
    h = _norm_mod(x_ref[...], g_ref[...], mod[:, 0:D], mod[:, D:2 * D]).astype(bf16)
    p = _dot(h, w1_ref[...])
    cq = _rmsnorm(p[:, 0:ML_RANK], qg_ref[...]).astype(bf16)
    ckv = _rmsnorm(p[:, ML_RANK:2 * ML_RANK], kvg_ref[...]).astype(bf16)
    cos = cos_ref[...]
    sin = sin_ref[...]
    krope = p[:, 2 * ML_RANK:2 * ML_RANK + LANE] * cos + p[:, 2 * ML_RANK + LANE:] * sin
    for c in range(ML_QW // ML_CK):
        lo = c * ML_CK
        qa = _dot(cq, wqa_ref[:, lo:lo + ML_CK])
        qb = _dot(cq, wqb_ref[:, lo:lo + ML_CK])
        kn = _dot(ckv, wk_ref[:, lo:lo + ML_CK])
        for e in range(ML_CK // LANE):
            ls = slice(e * LANE, (e + 1) * LANE)
            q_out[:, lo + e * LANE:lo + (e + 1) * LANE] = (
                (qa[:, ls] * cos + qb[:, ls] * sin) * ML_SCALE).astype(bf16)
            k_out[:, lo + e * LANE:lo + (e + 1) * LANE] = (kn[:, ls] + krope).astype(bf16)
    for c in range(ML_H * ML_V // ML_CK):
        lo = c * ML_CK
        v_out[:, lo:lo + ML_CK] = _dot(ckv, wv_ref[:, lo:lo + ML_CK]).astype(bf16)


def _mla_proj_call(x, modv, layer, g, w1, qg, kvg, wqa, wqb, wk, wv, cos_t, sin_t):
    rope_idx = lambda t: (jnp.where(t < N_LAT_TILES, t % TILES_PER_SEQ, TILES_PER_SEQ), 0)
    row = lambda w: pl.BlockSpec((TM, w), lambda t: (t, 0))
    return pl.pallas_call(
        _mla_proj_kernel,
        grid=(N_TILES,),
        in_specs=[row(D), _mod_spec(layer), _resident((1, D)), _resident((D, ML_P1)),
                  _resident((1, ML_RANK)), _resident((1, ML_RANK)),
                  _resident((ML_RANK, ML_QW)), _resident((ML_RANK, ML_QW)),
                  _resident((ML_RANK, ML_QW)), _resident((ML_RANK, ML_H * ML_V)),
                  pl.BlockSpec((TM, LANE), rope_idx), pl.BlockSpec((TM, LANE), rope_idx)],
        out_specs=[row(ML_QW), row(ML_QW), row(ML_H * ML_V)],
        out_shape=[jax.ShapeDtypeStruct((NT, ML_QW), bf16), jax.ShapeDtypeStruct((NT, ML_QW), bf16),
                   jax.ShapeDtypeStruct((NT, ML_H * ML_V), bf16)],
        compiler_params=pltpu.CompilerParams(vmem_limit_bytes=VMEM_LIMIT),
        name="mla_proj",
    )(x, modv, g.reshape(1, D), w1, qg.reshape(1, ML_RANK), kvg.reshape(1, ML_RANK),
      wqa, wqb, wk, wv, cos_t, sin_t)


ML_TQ = 256
ML_TK = 512


def _mla_attn_kernel(q_ref, kl_ref, kc_ref, vl_ref, vc_ref, out_ref):
    outs = []
    for e in range(2):
        ls = slice(e * LANE, (e + 1) * LANE)
        q = q_ref[:, ls]

        def chunk(k, v, carry):
            m, l, acc = carry
            s = _dot_nt(q, k)
            m_new = jnp.maximum(m, jnp.max(s, axis=-1, keepdims=True))
            alpha = jnp.exp(m - m_new)
            p = jnp.exp(s - m_new)
            l = alpha * l + jnp.sum(p, axis=-1, keepdims=True)
            acc = alpha * acc + _dot(p.astype(bf16), v)
            return m_new, l, acc

        def body(c, carry):
            off = pl.multiple_of(c * ML_TK, ML_TK)
            return chunk(kl_ref[pl.ds(off, ML_TK), ls], vl_ref[pl.ds(off, ML_TK), :], carry)

        init = (jnp.full((ML_TQ, 1), -jnp.inf, f32), jnp.zeros((ML_TQ, 1), f32),
                jnp.zeros((ML_TQ, LANE), f32))
        carry = lax.fori_loop(0, S // ML_TK, body, init)
        _, l, acc = chunk(kc_ref[:, ls], vc_ref[...], carry)
        outs.append(acc / l)
    lane = lax.broadcasted_iota(jnp.int32, (ML_TQ, LANE), 1)
    out_ref[...] = jnp.where(lane < 64, outs[0], outs[1]).astype(bf16)


def _mla_attn_call(q, k, v):
    nq = S // ML_TQ
    ctx_blk = T_LAT // CTX
    return pl.pallas_call(
        _mla_attn_kernel,
        grid=(B, HP, nq),
        in_specs=[
            pl.BlockSpec((ML_TQ, 2 * LANE), lambda b, hp, i: (b * nq + i, hp)),
            pl.BlockSpec((S, 2 * LANE), lambda b, hp, i: (b, hp)),
            pl.BlockSpec((CTX, 2 * LANE), lambda b, hp, i: (ctx_blk + b, hp)),
            pl.BlockSpec((S, LANE), lambda b, hp, i: (b, hp)),
            pl.BlockSpec((CTX, LANE), lambda b, hp, i: (ctx_blk + b, hp)),
        ],
        out_specs=pl.BlockSpec((ML_TQ, LANE), lambda b, hp, i: (b * nq + i, hp)),
        out_shape=jax.ShapeDtypeStruct((T_LAT, D), bf16),
        compiler_params=pltpu.CompilerParams(vmem_limit_bytes=VMEM_LIMIT),
        name="mla_attn",
    )(q, k, k, v, v)


def _mla_weights(w_dq, w_uq, w_dkv, w_ukv):
    z = lambda *s: jnp.zeros(s, f32)
    uq = w_uq.reshape(ML_RANK, ML_H, ML_NOPE + ML_ROPE)
    nope, r1, r2 = uq[..., :ML_NOPE], uq[..., ML_NOPE:ML_NOPE + 16], uq[..., ML_NOPE + 16:]
    pad = z(ML_RANK, ML_H, LANE - ML_NOPE - ML_ROPE)
    wqa = jnp.concatenate([nope, r1, r2, pad], axis=-1).reshape(ML_RANK, ML_QW)
    wqb = jnp.concatenate([jnp.zeros_like(nope), -r2, r1, pad], axis=-1).reshape(ML_RANK, ML_QW)
    ukv = w_ukv.reshape(ML_RANK, ML_H, ML_NOPE + ML_V)
    wk = jnp.concatenate([ukv[..., :ML_NOPE], z(ML_RANK, ML_H, LANE - ML_NOPE)], axis=-1).reshape(ML_RANK, ML_QW)
    wv = ukv[..., ML_NOPE:].reshape(ML_RANK, ML_H * ML_V)
    wr = w_dkv[:, ML_RANK:]
    rope1 = jnp.concatenate([z(D, ML_NOPE), wr, z(D, LANE - ML_NOPE - ML_ROPE)], axis=-1)
    rope2 = jnp.concatenate([z(D, ML_NOPE), -wr[:, 16:], wr[:, :16], z(D, LANE - ML_NOPE - ML_ROPE)], axis=-1)
    w1 = jnp.concatenate([w_dq, w_dkv[:, :ML_RANK], rope1, rope2], axis=-1)
    return tuple(w.astype(bf16) for w in (w1, wqa, wqb, wk, wv))


def _rope_tables():
    n_freq = ML_ROPE // 4
    freq = 10000.0 ** (-jnp.arange(n_freq, dtype=f32) / n_freq)
    t = jnp.arange(S)
    row = (t // GRID_W).astype(f32)
    col = (t % GRID_W).astype(f32)
    ang = jnp.concatenate([row[:, None] * freq, col[:, None] * freq], axis=-1)
    cos, sin = jnp.cos(ang), jnp.sin(ang)
    pad = jnp.zeros((S, LANE - ML_NOPE - ML_ROPE), f32)
    cos_l = jnp.concatenate([jnp.ones((S, ML_NOPE), f32), cos, cos, pad], axis=-1)
    sin_l = jnp.concatenate([jnp.zeros((S, ML_NOPE), f32), sin, sin, pad], axis=-1)
    cos_c = jnp.concatenate([jnp.ones((TM, ML_NOPE + ML_ROPE), f32), pad[:TM]], axis=-1)
    return jnp.concatenate([cos_l, cos_c], axis=0), jnp.concatenate([sin_l, jnp.zeros((TM, LANE), f32)], axis=0)


FN_G = 4
FN_C = D // FN_G
FN_T = 1024


def _fnet_chan_kernel(x_ref, mod_ref, g_ref, cc_ref, sc_ref, a_out, b_out):
    mod = mod_ref[...]
    h = _norm_mod(x_ref[...], g_ref[...], mod[:, 0:D], mod[:, D:2 * D]).astype(bf16)
    for gi in range(FN_G):
        ls = slice(gi * FN_C, (gi + 1) * FN_C)
        a_out[:, ls] = _dot(h[:, ls], cc_ref[...]).astype(bf16)
        b_out[:, ls] = _dot(h[:, ls], sc_ref[...]).astype(bf16)


def _fnet_chan_call(x, modv, layer, g, cc, sc):
    row = pl.BlockSpec((TM, D), lambda t: (t, 0))
    return pl.pallas_call(
        _fnet_chan_kernel,
        grid=(N_LAT_TILES,),
        in_specs=[row, _mod_spec(layer), _resident((1, D)), _resident((FN_C, FN_C)), _resident((FN_C, FN_C))],
        out_specs=[row, row],
        out_shape=[jax.ShapeDtypeStruct((T_LAT, D), bf16)] * 2,
        compiler_params=pltpu.CompilerParams(vmem_limit_bytes=VMEM_LIMIT),
        name="fnet_chan",
    )(x, modv, g.reshape(1, D), cc, sc)


def _fnet_seq_kernel(cs_ref, ns_ref, a_ref, b_ref, out_ref, acc_ref):
    n = pl.program_id(2)

    @pl.when(n == 0)
    def _():
        acc_ref[...] = jnp.zeros_like(acc_ref)

    acc_ref[...] += _dot(cs_ref[...], a_ref[...]) + _dot(ns_ref[...], b_ref[...])

    @pl.when(n == pl.num_programs(2) - 1)
    def _():
        out_ref[...] = (acc_ref[...] * (1.0 / math.sqrt(S * FN_C))).astype(bf16)


def _fnet_seq_call(cs, ns, a, b):
    nk = S // FN_T
    mat = pl.BlockSpec((FN_T, FN_T), lambda k, bb, n: (k, n))
    dat = pl.BlockSpec((FN_T, D), lambda k, bb, n: (bb * nk + n, 0))
    return pl.pallas_call(
        _fnet_seq_kernel,
        grid=(nk, B, nk),
        in_specs=[mat, mat, dat, dat],
        out_specs=pl.BlockSpec((FN_T, D), lambda k, bb, n: (bb * nk + k, 0)),
        out_shape=jax.ShapeDtypeStruct((T_LAT, D), bf16),
        scratch_shapes=[pltpu.VMEM((FN_T, D), f32)],
        compiler_params=pltpu.CompilerParams(vmem_limit_bytes=VMEM_LIMIT),
        name="fnet_seq",
    )(cs, ns, a, b)


def _dft_tables(n):
    j = lax.broadcasted_iota(jnp.int32, (n, n), 0)
    k = lax.broadcasted_iota(jnp.int32, (n, n), 1)
    ang = jnp.bitwise_and(j * k, n - 1).astype(f32) * (2.0 * math.pi / n)
    return jnp.cos(ang).astype(bf16), (-jnp.sin(ang)).astype(bf16)


def kernel(x, c, ctx, c_ctx, mod_w, mod_b, mix_norm_g, ffn_norm_g, conv_w_in, conv_w, conv_w_out,
           nat_w_qkv, nat_rpb, nat_w_o, mla_w_dq, mla_q_norm_g, mla_w_uq, mla_w_dkv, mla_kv_norm_g,
           mla_w_ukv, mla_w_o, fnet_w_o, ffn_w_in, ffn_w_out, final_norm_g):
    cc8 = jnp.concatenate([c, c_ctx[None, :], jnp.zeros((8 - B - 1, D), f32)], axis=0)
    modv = _mod_call(cc8, mod_w, mod_b).reshape(DEPTH * 8, 1, N_MOD * D)

    w_in = ffn_w_in.astype(bf16)
    w_out = ffn_w_out.astype(bf16)
    xs = jnp.concatenate([x.reshape(T_LAT, D), ctx.reshape(T_CTX, D)], axis=0)

    cw8 = jnp.concatenate([conv_w[0], jnp.zeros((5, D), f32)], axis=0)
    bz = _conv_call(xs, modv, 0, mix_norm_g[0], conv_w_in[0].astype(bf16), cw8)
    xs = _tail_call(xs, bz, modv, 0, conv_w_out[0].astype(bf16), ffn_norm_g[0], w_in[0], w_out[0], N_TILES)

    qkv = _qkv_call(xs, modv, 1, mix_norm_g[1], nat_w_qkv[0].astype(bf16))
    o = _nat_call(qkv, _nat_bias_table(nat_rpb[0]))
    o = _ctx_attn_call(qkv, o)
    xs = _tail_call(xs, o, modv, 1, nat_w_o[0].astype(bf16), ffn_norm_g[1], w_in[1], w_out[1], N_TILES)

    w1, wqa, wqb, wk, wv = _mla_weights(mla_w_dq[0], mla_w_uq[0], mla_w_dkv[0], mla_w_ukv[0])
    cos_t, sin_t = _rope_tables()
    q, k, v = _mla_proj_call(xs, modv, 2, mix_norm_g[2], w1, mla_q_norm_g[0], mla_kv_norm_g[0],
                             wqa, wqb, wk, wv, cos_t, sin_t)
    o = _mla_attn_call(q, k, v)
    xl = _tail_call(xs, o, modv, 2, mla_w_o[0].astype(bf16), ffn_norm_g[2], w_in[2], w_out[2], N_LAT_TILES)

    cc_t, ns_c = _dft_tables(FN_C)
    cs_t, ns_s = _dft_tables(S)
    a, b = _fnet_chan_call(xl, modv, 3, mix_norm_g[3], cc_t, (-ns_c.astype(f32)).astype(bf16))
    o = _fnet_seq_call(cs_t, ns_s, a, b)
    out = _tail_call(xl, o, modv, 3, fnet_w_o[0].astype(bf16), ffn_norm_g[3], w_in[3], w_out[3],
                     N_LAT_TILES, final_g=final_norm_g)
    return out.reshape(B, S, D)
```

```python
import functools
import math

import numpy as np
import jax
import jax.numpy as jnp
from jax import lax
from jax.experimental import pallas as pl
from jax.experimental.pallas import tpu as pltpu

D = 1024
B = 4
S = 4096
CTX = 256
DEPTH = 4
GRID_W = 64
N_MOD = 6
F = 2816
EPS = 1e-6

T_LAT = B * S
T_CTX = B * CTX
NT = T_LAT + T_CTX

NA_H = 16
NA_KR = 8
NA_KC = 16
ML_H = 16
ML_RANK = 256
ML_NOPE = 64
ML_ROPE = 32
ML_V = 64
HP = 8
LANE = 128
MASK_NEG = -1e30

TM = 512
N_TILES = NT // TM
N_LAT_TILES = T_LAT // TM
TILES_PER_SEQ = S // TM
FFN_CK = 256
VMEM_LIMIT = 56 * 1024 * 1024

f32 = jnp.float32
bf16 = jnp.bfloat16


def _resident(shape):
    nd = len(shape)
    return pl.BlockSpec(shape, lambda *_: (0,) * nd, pipeline_mode=pl.Buffered(1))


def _mod_spec(layer):
    return pl.BlockSpec((None, 1, N_MOD * D), lambda t: (layer * 8 + t // TILES_PER_SEQ, 0, 0))


def _rmsnorm(x, g):
    ms = jnp.mean(x * x, axis=-1, keepdims=True)
    return (x * lax.rsqrt(ms + EPS)) * g


def _norm_mod(x, g, shift, scale):
    return _rmsnorm(x, g) * (1.0 + scale) + shift


def _dot(a, b):
    return jnp.dot(a, b, preferred_element_type=f32)


def _dot_nt(a, b):
    return lax.dot_general(a, b, (((1,), (1,)), ((), ())), preferred_element_type=f32)


def _mod_kernel(cc_ref, w_ref, b_ref, out_ref):
    cc = cc_ref[...]
    s = (cc * jax.nn.sigmoid(cc)).astype(bf16)
    out_ref[...] = _dot(s, w_ref[...].astype(bf16)) + b_ref[...]


def _mod_call(cc8, mod_w, mod_b):
    tn = 1536
    return pl.pallas_call(
        _mod_kernel,
        grid=(DEPTH, N_MOD * D // tn),
        in_specs=[
            pl.BlockSpec((8, D), lambda l, j: (0, 0)),
            pl.BlockSpec((None, D, tn), lambda l, j: (l, 0, j)),
            pl.BlockSpec((None, 1, tn), lambda l, j: (l, 0, j)),
        ],
        out_specs=pl.BlockSpec((None, 8, tn), lambda l, j: (l, 0, j)),
        out_shape=jax.ShapeDtypeStruct((DEPTH, 8, N_MOD * D), f32),
        compiler_params=pltpu.CompilerParams(vmem_limit_bytes=VMEM_LIMIT),
        name="mod_vectors",
    )(cc8, mod_w, mod_b.reshape(DEPTH, 1, N_MOD * D))


def _tail_kernel(x_ref, o_ref, mod_ref, wo_ref, g_ref, win_ref, wout_ref, *rest, final):
    if final:
        fg_ref, out_ref, acc_ref = rest
    else:
        out_ref, acc_ref = rest
    mod = mod_ref[...]
    g1 = mod[:, 2 * D:3 * D]
    sh2 = mod[:, 3 * D:4 * D]
    sc2 = mod[:, 4 * D:5 * D]
    g2 = mod[:, 5 * D:6 * D]
    x1 = x_ref[...] + g1 * _dot(o_ref[...], wo_ref[...])
    h = _norm_mod(x1, g_ref[...], sh2, sc2).astype(bf16)
    for c in range(F // FFN_CK):
        lo = c * FFN_CK
        gate = _dot(h, win_ref[:, lo:lo + FFN_CK])
        up = _dot(h, win_ref[:, F + lo:F + lo + FFN_CK])
        a = (gate * jax.nn.sigmoid(gate) * up).astype(bf16)
        y = _dot(a, wout_ref[lo:lo + FFN_CK, :])
        if c == 0:
            acc_ref[...] = y
        else:
            acc_ref[...] += y
    x2 = x1 + g2 * acc_ref[...]
    if final:
        x2 = _rmsnorm(x2, fg_ref[...])
    out_ref[...] = x2


def _tail_call(x, o, modv, layer, wo, g, win, wout, n_tiles, final_g=None):
    final = final_g is not None
    row = lambda w: pl.BlockSpec((TM, w), lambda t: (t, 0))
    in_specs = [row(D), row(D), _mod_spec(layer), _resident((D, D)), _resident((1, D)),
                _resident((D, 2 * F)), _resident((F, D))]
    args = [x, o, modv, wo, g.reshape(1, D), win, wout]
    if final:
        in_specs.append(_resident((1, D)))
        args.append(final_g.reshape(1, D))
    return pl.pallas_call(
        functools.partial(_tail_kernel, final=final),
        grid=(n_tiles,),
        in_specs=in_specs,
        out_specs=row(D),
        out_shape=jax.ShapeDtypeStruct((n_tiles * TM, D), f32),
        scratch_shapes=[pltpu.VMEM((TM, D), f32)],
        compiler_params=pltpu.CompilerParams(vmem_limit_bytes=VMEM_LIMIT),
        name=f"tail_l{layer}",
    )(*args)


CONV_HALO = 16
CONV_CK = 256


def _conv_kernel(xp_ref, x_ref, xn_ref, mod_ref, g_ref, win_ref, cw_ref, out_ref):
    t = pl.program_id(0)
    mod = mod_ref[...]
    sh1 = mod[:, 0:D]
    sc1 = mod[:, D:2 * D]
    g = g_ref[...]
    h = jnp.concatenate(
        [_norm_mod(r[...], g, sh1, sc1).astype(bf16) for r in (xp_ref, x_ref, xn_ref)], axis=0)
    rows = TM + 2 * CONV_HALO
    seq_len = jnp.where(t < N_LAT_TILES, S, CTX)
    grow = t * TM - CONV_HALO + lax.broadcasted_iota(jnp.int32, (rows, CONV_CK), 0)
    pos = jnp.bitwise_and(grow, seq_len - 1)
    has_prev = pos != 0
    has_next = pos != seq_len - 1
    for j in range(D // CONV_CK):
        lo = j * CONV_CK
        bg = _dot(h, win_ref[:, lo:lo + CONV_CK])
        cg = _dot(h, win_ref[:, D + lo:D + lo + CONV_CK])
        v = _dot(h, win_ref[:, 2 * D + lo:2 * D + lo + CONV_CK])
        u = cg * v
        u_prev = jnp.where(has_prev, pltpu.roll(u, 1, axis=0), 0.0)
        u_next = jnp.where(has_next, pltpu.roll(u, rows - 1, axis=0), 0.0)
        z = (cw_ref[0:1, lo:lo + CONV_CK] * u_prev + cw_ref[1:2, lo:lo + CONV_CK] * u
             + cw_ref[2:3, lo:lo + CONV_CK] * u_next)
        bz = bg * z
        out_ref[:, lo:lo + CONV_CK] = bz[CONV_HALO:CONV_HALO + TM].astype(bf16)


def _conv_call(x, modv, layer, g, win, cw):
    hb = TM // CONV_HALO
    last = NT // CONV_HALO - 1
    return pl.pallas_call(
        _conv_kernel,
        grid=(N_TILES,),
        in_specs=[
            pl.BlockSpec((CONV_HALO, D), lambda t: (jnp.maximum(t * hb - 1, 0), 0)),
            pl.BlockSpec((TM, D), lambda t: (t, 0)),
            pl.BlockSpec((CONV_HALO, D), lambda t: (jnp.minimum((t + 1) * hb, last), 0)),
            _mod_spec(layer), _resident((1, D)), _resident((D, 3 * D)), _resident((8, D)),
        ],
        out_specs=pl.BlockSpec((TM, D), lambda t: (t, 0)),
        out_shape=jax.ShapeDtypeStruct((NT, D), bf16),
        compiler_params=pltpu.CompilerParams(vmem_limit_bytes=VMEM_LIMIT),
        name="conv_mixer",
    )(x, x, x, modv, g.reshape(1, D), win, cw)


QKV_CK = 512
NA_SCALE = 0.125


def _qkv_kernel(x_ref, mod_ref, g_ref, w_ref, out_ref):
    mod = mod_ref[...]
    h = _norm_mod(x_ref[...], g_ref[...], mod[:, 0:D], mod[:, D:2 * D]).astype(bf16)
    for c in range(3 * D // QKV_CK):
        lo = c * QKV_CK
        r = _dot(h, w_ref[:, lo:lo + QKV_CK])
        if lo < D:
            r = r * NA_SCALE
        out_ref[:, lo:lo + QKV_CK] = r.astype(bf16)


def _qkv_call(x, modv, layer, g, w):
    return pl.pallas_call(
        _qkv_kernel,
        grid=(N_TILES,),
        in_specs=[pl.BlockSpec((TM, D), lambda t: (t, 0)), _mod_spec(layer),
                  _resident((1, D)), _resident((D, 3 * D))],
        out_specs=pl.BlockSpec((TM, 3 * D), lambda t: (t, 0)),
        out_shape=jax.ShapeDtypeStruct((NT, 3 * D), bf16),
        compiler_params=pltpu.CompilerParams(vmem_limit_bytes=VMEM_LIMIT),
        name="nat_qkv",
    )(x, modv, g.reshape(1, D), w)


def _stack_pair(q2):
    lane = lax.broadcasted_iota(jnp.int32, q2.shape, 1)
    zero = jnp.zeros_like(q2)
    return jnp.concatenate([jnp.where(lane < 64, q2, zero), jnp.where(lane >= 64, q2, zero)], axis=0)


def _unstack_pair(o, n):
    lane = lax.broadcasted_iota(jnp.int32, (n, LANE), 1)
    return jnp.where(lane < 64, o[:n], o[n:])


def _nat_kernel(q_ref, kb_ref, vb_ref, kc_ref, vc_ref, bias_ref, out_ref):
    r = pl.program_id(1)
    rows = S // GRID_W
    r0 = jnp.clip(r - NA_KR // 2, 0, rows - NA_KR)
    off = pl.multiple_of(r0 * GRID_W, GRID_W)
    nband = NA_KR * GRID_W
    lanes = [slice(hp * LANE, (hp + 1) * LANE) for hp in range(HP)]
    scores = []
    for hp, ls in enumerate(lanes):
        qst = _stack_pair(q_ref[:, ls])
        s_w = _dot_nt(qst, kb_ref[pl.ds(off, nband), ls]) + bias_ref[hp]
        s_c = _dot_nt(qst, kc_ref[:, ls])
        scores.append((s_w, s_c))
    probs = []
    for s_w, s_c in scores:
        m = jnp.maximum(jnp.max(s_w, axis=-1, keepdims=True), jnp.max(s_c, axis=-1, keepdims=True))
        p_w = jnp.exp(s_w - m)
        p_c = jnp.exp(s_c - m)
        l = jnp.sum(p_w, axis=-1, keepdims=True) + jnp.sum(p_c, axis=-1, keepdims=True)
        probs.append((p_w.astype(bf16), p_c.astype(bf16), l))
    for (p_w, p_c, l), ls in zip(probs, lanes):
        o = _dot(p_w, vb_ref[pl.ds(off, nband), ls]) + _dot(p_c, vc_ref[:, ls])
        out_ref[:, ls] = _unstack_pair(o / l, GRID_W).astype(bf16)


def _nat_call(qkv, bias):
    rows = S // GRID_W

    def bias_idx(b, r):
        r0 = jnp.clip(r - NA_KR // 2, 0, rows - NA_KR)
        return (r0 - r + NA_KR - 1, 0, 0, 0)

    ctx_blk = T_LAT // CTX
    return pl.pallas_call(
        _nat_kernel,
        grid=(B, rows),
        in_specs=[
            pl.BlockSpec((GRID_W, D), lambda b, r: (b * rows + r, 0)),
            pl.BlockSpec((S, D), lambda b, r: (b, 1)),
            pl.BlockSpec((S, D), lambda b, r: (b, 2)),
            pl.BlockSpec((CTX, D), lambda b, r: (ctx_blk + b, 1)),
            pl.BlockSpec((CTX, D), lambda b, r: (ctx_blk + b, 2)),
            pl.BlockSpec((None, HP, 2 * GRID_W, NA_KR * GRID_W), bias_idx),
        ],
        out_specs=pl.BlockSpec((GRID_W, D), lambda b, r: (b * rows + r, 0)),
        out_shape=jax.ShapeDtypeStruct((NT, D), bf16),
        compiler_params=pltpu.CompilerParams(vmem_limit_bytes=VMEM_LIMIT),
        name="nat_attn",
    )(qkv, qkv, qkv, qkv, qkv, bias)


def _ctx_attn_kernel(q_ref, k_ref, v_ref, o_in_ref, out_ref):
    del o_in_ref
    for hp in range(HP):
        ls = slice(hp * LANE, (hp + 1) * LANE)
        qst = _stack_pair(q_ref[:, ls])
        s = _dot_nt(qst, k_ref[:, ls])
        m = jnp.max(s, axis=-1, keepdims=True)
        p = jnp.exp(s - m)
        l = jnp.sum(p, axis=-1, keepdims=True)
        o = _dot(p.astype(bf16), v_ref[:, ls])
        out_ref[:, ls] = _unstack_pair(o / l, CTX).astype(bf16)


def _ctx_attn_call(qkv, o):
    ctx_blk = T_LAT // CTX
    return pl.pallas_call(
        _ctx_attn_kernel,
        grid=(B,),
        in_specs=[
            pl.BlockSpec((CTX, D), lambda b: (ctx_blk + b, 0)),
            pl.BlockSpec((CTX, D), lambda b: (ctx_blk + b, 1)),
            pl.BlockSpec((CTX, D), lambda b: (ctx_blk + b, 2)),
            pl.BlockSpec(memory_space=pl.ANY),
        ],
        out_specs=pl.BlockSpec((CTX, D), lambda b: (ctx_blk + b, 0)),
        out_shape=jax.ShapeDtypeStruct((NT, D), bf16),
        input_output_aliases={3: 0},
        compiler_params=pltpu.CompilerParams(vmem_limit_bytes=VMEM_LIMIT),
        name="nat_ctx_attn",
    )(qkv, qkv, qkv, o)


def _nat_bias_table(rpb):
    cols = np.arange(GRID_W)
    c0 = np.clip(cols - NA_KC // 2, 0, GRID_W - NA_KC)
    j = cols[None, :]
    q = cols[:, None]
    valid = (j >= c0[:, None]) & (j < c0[:, None] + NA_KC)
    rel = np.clip(j - q + NA_KC - 1, 0, 2 * NA_KC - 2)
    dense = jnp.where(valid[None, None], rpb[:, :, rel], MASK_NEG)
    slabs = []
    for d0 in range(NA_KR):
        band = dense[:, d0:d0 + NA_KR]
        slabs.append(band.transpose(0, 2, 1, 3).reshape(NA_H, GRID_W, NA_KR * GRID_W))
    tab = jnp.stack(slabs, axis=0)
    return tab.reshape(NA_KR, HP, 2 * GRID_W, NA_KR * GRID_W).astype(f32)


ML_SCALE = (ML_NOPE + ML_ROPE) ** -0.5 * math.log2(math.e)
ML_QW = ML_H * LANE
ML_P1 = 2 * ML_RANK + 2 * LANE
ML_CK = 256


def _mla_proj_kernel(x_ref, mod_ref, g_ref, w1_ref, qg_ref, kvg_ref, wqa_ref, wqb_ref, wk_ref, wv_ref,
                     cos_ref, sin_ref, q_out, k_out, v_out):
    mod = mod_ref[...]
    h = _norm_mod(x_ref[...], g_ref[...], mod[:, 0:D], mod[:, D:2 * D]).astype(bf16)
    p = _dot(h, w1_ref[...])
    cq = _rmsnorm(p[:, 0:ML_RANK], qg_ref[...]).astype(bf16)
    ckv = _rmsnorm(p[:, ML_RANK:2 * ML_RANK], kvg_ref[...]).astype(bf16)
    cos = cos_ref[...]
    sin = sin_ref[...]
    krope = p[:, 2 * ML_RANK:2 * ML_RANK + LANE] * cos + p[:, 2 * ML_RANK + LANE:] * sin
    for c in range(ML_QW // ML_CK):
        lo = c * ML_CK
        qa = _dot(cq, wqa_ref[:, lo:lo + ML_CK])
        qb = _dot(cq, wqb_ref[:, lo:lo + ML_CK])
        kn = _dot(ckv, wk_ref[:, lo:lo + ML_CK])
        for e in range(ML_CK // LANE):
            ls = slice(e * LANE, (e + 1) * LANE)
            q_out[:, lo + e * LANE:lo + (e + 1) * LANE] = (
                (qa[:, ls] * cos + qb[:, ls] * sin) * ML_SCALE).astype(bf16)
            k_out[:, lo + e * LANE:lo + (e + 1) * LANE] = (kn[:, ls] + krope).astype(bf16)
    for c in range(ML_H * ML_V // ML_CK):
        lo = c * ML_CK
        v_out[:, lo:lo + ML_CK] = _dot(ckv, wv_ref[:, lo:lo + ML_CK]).astype(bf16)


def _mla_proj_call(x, modv, layer, g, w1, qg, kvg, wqa, wqb, wk, wv, cos_t, sin_t):
    rope_idx = lambda t: (jnp.where(t < N_LAT_TILES, t % TILES_PER_SEQ, TILES_PER_SEQ), 0)
    row = lambda w: pl.BlockSpec((TM, w), lambda t: (t, 0))
    return pl.pallas_call(
        _mla_proj_kernel,
        grid=(N_TILES,),
        in_specs=[row(D), _mod_spec(layer), _resident((1, D)), _resident((D, ML_P1)),
                  _resident((1, ML_RANK)), _resident((1, ML_RANK)),
                  _resident((ML_RANK, ML_QW)), _resident((ML_RANK, ML_QW)),
                  _resident((ML_RANK, ML_QW)), _resident((ML_RANK, ML_H * ML_V)),
                  pl.BlockSpec((TM, LANE), rope_idx), pl.BlockSpec((TM, LANE), rope_idx)],
        out_specs=[row(ML_QW), row(ML_QW), row(ML_H * ML_V)],
        out_shape=[jax.ShapeDtypeStruct((NT, ML_QW), bf16), jax.ShapeDtypeStruct((NT, ML_QW), bf16),
                   jax.ShapeDtypeStruct((NT, ML_H * ML_V), bf16)],
        compiler_params=pltpu.CompilerParams(vmem_limit_bytes=VMEM_LIMIT),
        name="mla_proj",
    )(x, modv, g.reshape(1, D), w1, qg.reshape(1, ML_RANK), kvg.reshape(1, ML_RANK),
      wqa, wqb, wk, wv, cos_t, sin_t)


ML_TQ = 1024
ML_TK = 1024


def _mla_attn_kernel(q_ref, kl_ref, kc_ref, vl_ref, vc_ref, out_ref):
    qs = [q_ref[:, e * LANE:(e + 1) * LANE] for e in range(2)]
    n_lat = S // ML_TK

    def keys(c):
        return kl_ref[c * ML_TK:(c + 1) * ML_TK, :] if c < n_lat else kc_ref[...]

    def values(c):
        return vl_ref[c * ML_TK:(c + 1) * ML_TK, :] if c < n_lat else vc_ref[...]

    def update(e, c, carry):
        m, l, acc = carry
        s = _dot_nt(qs[e], keys(c)[:, e * LANE:(e + 1) * LANE])
        m_new = jnp.maximum(m, jnp.max(s, axis=-1, keepdims=True))
        alpha = jnp.exp2(m - m_new)
        p = jnp.exp2(s - m_new)
        l = alpha * l + jnp.sum(p, axis=-1, keepdims=True)
        acc = alpha * acc + _dot(p.astype(bf16), values(c))
        return m_new, l, acc

    init = (jnp.full((ML_TQ, 1), -jnp.inf, f32), jnp.zeros((ML_TQ, 1), f32),
            jnp.zeros((ML_TQ, LANE), f32))
    carry = [init, init]
    for c in range(n_lat + 1):
        carry = [update(e, c, carry[e]) for e in range(2)]
    (_, l0, a0), (_, l1, a1) = carry
    lane = lax.broadcasted_iota(jnp.int32, (ML_TQ, LANE), 1)
    out_ref[...] = jnp.where(lane < 64, a0 / l0, a1 / l1).astype(bf16)


def _mla_attn_call(q, k, v):
    nq = S // ML_TQ
    ctx_blk = T_LAT // CTX
    return pl.pallas_call(
        _mla_attn_kernel,
        grid=(B, HP, nq),
        in_specs=[
            pl.BlockSpec((ML_TQ, 2 * LANE), lambda b, hp, i: (b * nq + i, hp)),
            pl.BlockSpec((S, 2 * LANE), lambda b, hp, i: (b, hp)),
            pl.BlockSpec((CTX, 2 * LANE), lambda b, hp, i: (ctx_blk + b, hp)),
            pl.BlockSpec((S, LANE), lambda b, hp, i: (b, hp)),
            pl.BlockSpec((CTX, LANE), lambda b, hp, i: (ctx_blk + b, hp)),
        ],
        out_specs=pl.BlockSpec((ML_TQ, LANE), lambda b, hp, i: (b * nq + i, hp)),
        out_shape=jax.ShapeDtypeStruct((T_LAT, D), bf16),
        compiler_params=pltpu.CompilerParams(vmem_limit_bytes=VMEM_LIMIT),
        name="mla_attn",
    )(q, k, k, v, v)


def _mla_weights(w_dq, w_uq, w_dkv, w_ukv):
    z = lambda *s: jnp.zeros(s, f32)
    uq = w_uq.reshape(ML_RANK, ML_H, ML_NOPE + ML_ROPE)
    nope, r1, r2 = uq[..., :ML_NOPE], uq[..., ML_NOPE:ML_NOPE + 16], uq[..., ML_NOPE + 16:]
    pad = z(ML_RANK, ML_H, LANE - ML_NOPE - ML_ROPE)
    wqa = jnp.concatenate([nope, r1, r2, pad], axis=-1).reshape(ML_RANK, ML_QW)
    wqb = jnp.concatenate([jnp.zeros_like(nope), -r2, r1, pad], axis=-1).reshape(ML_RANK, ML_QW)
    ukv = w_ukv.reshape(ML_RANK, ML_H, ML_NOPE + ML_V)
    wk = jnp.concatenate([ukv[..., :ML_NOPE], z(ML_RANK, ML_H, LANE - ML_NOPE)], axis=-1).reshape(ML_RANK, ML_QW)
    wv = ukv[..., ML_NOPE:].reshape(ML_RANK, ML_H * ML_V)
    wr = w_dkv[:, ML_RANK:]
    rope1 = jnp.concatenate([z(D, ML_NOPE), wr, z(D, LANE - ML_NOPE - ML_ROPE)], axis=-1)
    rope2 = jnp.concatenate([z(D, ML_NOPE), -wr[:, 16:], wr[:, :16], z(D, LANE - ML_NOPE - ML_ROPE)], axis=-1)
    w1 = jnp.concatenate([w_dq, w_dkv[:, :ML_RANK], rope1, rope2], axis=-1)
    return tuple(w.astype(bf16) for w in (w1, wqa, wqb, wk, wv))


def _rope_tables():
    n_freq = ML_ROPE // 4
    freq = 10000.0 ** (-jnp.arange(n_freq, dtype=f32) / n_freq)
    t = jnp.arange(S)
    row = (t // GRID_W).astype(f32)
    col = (t % GRID_W).astype(f32)
    ang = jnp.concatenate([row[:, None] * freq, col[:, None] * freq], axis=-1)
    cos, sin = jnp.cos(ang), jnp.sin(ang)
    pad = jnp.zeros((S, LANE - ML_NOPE - ML_ROPE), f32)
    cos_l = jnp.concatenate([jnp.ones((S, ML_NOPE), f32), cos, cos, pad], axis=-1)
    sin_l = jnp.concatenate([jnp.zeros((S, ML_NOPE), f32), sin, sin, pad], axis=-1)
    cos_c = jnp.concatenate([jnp.ones((TM, ML_NOPE + ML_ROPE), f32), pad[:TM]], axis=-1)
    return jnp.concatenate([cos_l, cos_c], axis=0), jnp.concatenate([sin_l, jnp.zeros((TM, LANE), f32)], axis=0)


FN_G = 4
FN_C = D // FN_G
FN_T = 1024


def _fnet_chan_kernel(x_ref, mod_ref, g_ref, cc_ref, sc_ref, a_out, b_out):
    mod = mod_ref[...]
    h = _norm_mod(x_ref[...], g_ref[...], mod[:, 0:D], mod[:, D:2 * D]).astype(bf16)
    for gi in range(FN_G):
        ls = slice(gi * FN_C, (gi + 1) * FN_C)
        a_out[:, ls] = _dot(h[:, ls], cc_ref[...]).astype(bf16)
        b_out[:, ls] = _dot(h[:, ls], sc_ref[...]).astype(bf16)


def _fnet_chan_call(x, modv, layer, g, cc, sc):
    row = pl.BlockSpec((TM, D), lambda t: (t, 0))
    return pl.pallas_call(
        _fnet_chan_kernel,
        grid=(N_LAT_TILES,),
        in_specs=[row, _mod_spec(layer), _resident((1, D)), _resident((FN_C, FN_C)), _resident((FN_C, FN_C))],
        out_specs=[row, row],
        out_shape=[jax.ShapeDtypeStruct((T_LAT, D), bf16)] * 2,
        compiler_params=pltpu.CompilerParams(vmem_limit_bytes=VMEM_LIMIT),
        name="fnet_chan",
    )(x, modv, g.reshape(1, D), cc, sc)


def _fnet_seq_kernel(cs_ref, ns_ref, a_ref, b_ref, out_ref, acc_ref):
    n = pl.program_id(2)

    @pl.when(n == 0)
    def _():
        acc_ref[...] = jnp.zeros_like(acc_ref)

    acc_ref[...] += _dot(cs_ref[...], a_ref[...]) + _dot(ns_ref[...], b_ref[...])

    @pl.when(n == pl.num_programs(2) - 1)
    def _():
        out_ref[...] = (acc_ref[...] * (1.0 / math.sqrt(S * FN_C))).astype(bf16)


def _fnet_seq_call(cs, ns, a, b):
    nk = S // FN_T
    mat = pl.BlockSpec((FN_T, FN_T), lambda k, bb, n: (k, n))
    dat = pl.BlockSpec((FN_T, D), lambda k, bb, n: (bb * nk + n, 0))
    return pl.pallas_call(
        _fnet_seq_kernel,
        grid=(nk, B, nk),
        in_specs=[mat, mat, dat, dat],
        out_specs=pl.BlockSpec((FN_T, D), lambda k, bb, n: (bb * nk + k, 0)),
        out_shape=jax.ShapeDtypeStruct((T_LAT, D), bf16),
        scratch_shapes=[pltpu.VMEM((FN_T, D), f32)],
        compiler_params=pltpu.CompilerParams(vmem_limit_bytes=VMEM_LIMIT),
        name="fnet_seq",
    )(cs, ns, a, b)


def _dft_tables(n):
    j = lax.broadcasted_iota(jnp.int32, (n, n), 0)
    k = lax.broadcasted_iota(jnp.int32, (n, n), 1)
    ang = jnp.bitwise_and(j * k, n - 1).astype(f32) * (2.0 * math.pi / n)
    return jnp.cos(ang).astype(bf16), (-jnp.sin(ang)).astype(bf16)


def kernel(x, c, ctx, c_ctx, mod_w, mod_b, mix_norm_g, ffn_norm_g, conv_w_in, conv_w, conv_w_out,
           nat_w_qkv, nat_rpb, nat_w_o, mla_w_dq, mla_q_norm_g, mla_w_uq, mla_w_dkv, mla_kv_norm_g,
           mla_w_ukv, mla_w_o, fnet_w_o, ffn_w_in, ffn_w_out, final_norm_g):
    cc8 = jnp.concatenate([c, c_ctx[None, :], jnp.zeros((8 - B - 1, D), f32)], axis=0)
    modv = _mod_call(cc8, mod_w, mod_b).reshape(DEPTH * 8, 1, N_MOD * D)

    w_in = ffn_w_in.astype(bf16)
    w_out = ffn_w_out.astype(bf16)
    xs = jnp.concatenate([x.reshape(T_LAT, D), ctx.reshape(T_CTX, D)], axis=0)

    cw8 = jnp.concatenate([conv_w[0], jnp.zeros((5, D), f32)], axis=0)
    bz = _conv_call(xs, modv, 0, mix_norm_g[0], conv_w_in[0].astype(bf16), cw8)
    xs = _tail_call(xs, bz, modv, 0, conv_w_out[0].astype(bf16), ffn_norm_g[0], w_in[0], w_out[0], N_TILES)

    qkv = _qkv_call(xs, modv, 1, mix_norm_g[1], nat_w_qkv[0].astype(bf16))
    o = _nat_call(qkv, _nat_bias_table(nat_rpb[0]))
    o = _ctx_attn_call(qkv, o)
    xs = _tail_call(xs, o, modv, 1, nat_w_o[0].astype(bf16), ffn_norm_g[1], w_in[1], w_out[1], N_TILES)

    w1, wqa, wqb, wk, wv = _mla_weights(mla_w_dq[0], mla_w_uq[0], mla_w_dkv[0], mla_w_ukv[0])
    cos_t, sin_t = _rope_tables()
    q, k, v = _mla_proj_call(xs, modv, 2, mix_norm_g[2], w1, mla_q_norm_g[0], mla_kv_norm_g[0],
                             wqa, wqb, wk, wv, cos_t, sin_t)
    o = _mla_attn_call(q, k, v)
    xl = _tail_call(xs, o, modv, 2, mla_w_o[0].astype(bf16), ffn_norm_g[2], w_in[2], w_out[2], N_LAT_TILES)

    cc_t, ns_c = _dft_tables(FN_C)
    cs_t, ns_s = _dft_tables(S)
    a, b = _fnet_chan_call(xl, modv, 3, mix_norm_g[3], cc_t, (-ns_c.astype(f32)).astype(bf16))
    o = _fnet_seq_call(cs_t, ns_s, a, b)
    out = _tail_call(xl, o, modv, 3, fnet_w_o[0].astype(bf16), ffn_norm_g[3], w_in[3], w_out[3],
                     N_LAT_TILES, final_g=final_norm_g)
    return out.reshape(B, S, D)
```

```python
import functools
import math

import numpy as np
import jax
import jax.numpy as jnp
from jax import lax
from jax.experimental import pallas as pl
from jax.experimental.pallas import tpu as pltpu

D = 1024
B = 4
S = 4096
CTX = 256
DEPTH = 4
GRID_W = 64
N_MOD = 6
F = 2816
EPS = 1e-6

T_LAT = B * S
T_CTX = B * CTX
NT = T_LAT + T_CTX

NA_H = 16
NA_KR = 8
NA_KC = 16
ML_H = 16
ML_RANK = 256
ML_NOPE = 64
ML_ROPE = 32
ML_V = 64
HP = 8
LANE = 128
MASK_NEG = -1e30

TM = 512
N_TILES = NT // TM
N_LAT_TILES = T_LAT // TM
TILES_PER_SEQ = S // TM
FFN_CK = 256
VMEM_LIMIT = 56 * 1024 * 1024

f32 = jnp.float32
bf16 = jnp.bfloat16


def _resident(shape):
    nd = len(shape)
    return pl.BlockSpec(shape, lambda *_: (0,) * nd, pipeline_mode=pl.Buffered(1))


def _mod_spec(layer):
    return pl.BlockSpec((None, 1, N_MOD * D), lambda t: (layer * 8 + t // TILES_PER_SEQ, 0, 0))


def _rmsnorm(x, g):
    ms = jnp.mean(x * x, axis=-1, keepdims=True)
    return (x * lax.rsqrt(ms + EPS)) * g


def _norm_mod(x, g, shift, scale):
    return _rmsnorm(x, g) * (1.0 + scale) + shift


def _dot(a, b):
    return jnp.dot(a, b, preferred_element_type=f32)


def _dot_nt(a, b):
    return lax.dot_general(a, b, (((1,), (1,)), ((), ())), preferred_element_type=f32)


def _mod_kernel(cc_ref, w_ref, b_ref, out_ref):
    cc = cc_ref[...]
    s = (cc * jax.nn.sigmoid(cc)).astype(bf16)
    out_ref[...] = _dot(s, w_ref[...].astype(bf16)) + b_ref[...]


def _mod_call(cc8, mod_w, mod_b):
    tn = 1536
    return pl.pallas_call(
        _mod_kernel,
        grid=(DEPTH, N_MOD * D // tn),
        in_specs=[
            pl.BlockSpec((8, D), lambda l, j: (0, 0)),
            pl.BlockSpec((None, D, tn), lambda l, j: (l, 0, j)),
            pl.BlockSpec((None, 1, tn), lambda l, j: (l, 0, j)),
        ],
        out_specs=pl.BlockSpec((None, 8, tn), lambda l, j: (l, 0, j)),
        out_shape=jax.ShapeDtypeStruct((DEPTH, 8, N_MOD * D), f32),
        compiler_params=pltpu.CompilerParams(vmem_limit_bytes=VMEM_LIMIT),
        name="mod_vectors",
    )(cc8, mod_w, mod_b.reshape(DEPTH, 1, N_MOD * D))


def _tail_kernel(x_ref, o_ref, mod_ref, wo_ref, g_ref, win_ref, wout_ref, *rest, final):
    if final:
        fg_ref, out_ref, acc_ref = rest
    else:
        out_ref, acc_ref = rest
    mod = mod_ref[...]
    g1 = mod[:, 2 * D:3 * D]
    sh2 = mod[:, 3 * D:4 * D]
    sc2 = mod[:, 4 * D:5 * D]
    g2 = mod[:, 5 * D:6 * D]
    x1 = x_ref[...] + g1 * _dot(o_ref[...], wo_ref[...])
    h = _norm_mod(x1, g_ref[...], sh2, sc2).astype(bf16)
    for c in range(F // FFN_CK):
        lo = c * FFN_CK
        gate = _dot(h, win_ref[:, lo:lo + FFN_CK])
        up = _dot(h, win_ref[:, F + lo:F + lo + FFN_CK])
        a = (gate * jax.nn.sigmoid(gate) * up).astype(bf16)
        y = _dot(a, wout_ref[lo:lo + FFN_CK, :])
        if c == 0:
            acc_ref[...] = y
        else:
            acc_ref[...] += y
    x2 = x1 + g2 * acc_ref[...]
    if final:
        x2 = _rmsnorm(x2, fg_ref[...])
    out_ref[...] = x2


def _tail_call(x, o, modv, layer, wo, g, win, wout, n_tiles, final_g=None):
    final = final_g is not None
    row = lambda w: pl.BlockSpec((TM, w), lambda t: (t, 0))
    in_specs = [row(D), row(D), _mod_spec(layer), _resident((D, D)), _resident((1, D)),
                _resident((D, 2 * F)), _resident((F, D))]
    args = [x, o, modv, wo, g.reshape(1, D), win, wout]
    if final:
        in_specs.append(_resident((1, D)))
        args.append(final_g.reshape(1, D))
    return pl.pallas_call(
        functools.partial(_tail_kernel, final=final),
        grid=(n_tiles,),
        in_specs=in_specs,
        out_specs=row(D),
        out_shape=jax.ShapeDtypeStruct((n_tiles * TM, D), f32),
        scratch_shapes=[pltpu.VMEM((TM, D), f32)],
        compiler_params=pltpu.CompilerParams(vmem_limit_bytes=VMEM_LIMIT),
        name=f"tail_l{layer}",
    )(*args)


CONV_HALO = 16
CONV_CK = 256


def _conv_kernel(xp_ref, x_ref, xn_ref, mod_ref, g_ref, win_ref, cw_ref, out_ref):
    t = pl.program_id(0)
    mod = mod_ref[...]
    sh1 = mod[:, 0:D]
    sc1 = mod[:, D:2 * D]
    g = g_ref[...]
    h = jnp.concatenate(
        [_norm_mod(r[...], g, sh1, sc1).astype(bf16) for r in (xp_ref, x_ref, xn_ref)], axis=0)
    rows = TM + 2 * CONV_HALO
    seq_len = jnp.where(t < N_LAT_TILES, S, CTX)
    grow = t * TM - CONV_HALO + lax.broadcasted_iota(jnp.int32, (rows, CONV_CK), 0)
    pos = jnp.bitwise_and(grow, seq_len - 1)
    has_prev = pos != 0
    has_next = pos != seq_len - 1
    for j in range(D // CONV_CK):
        lo = j * CONV_CK
        bg = _dot(h, win_ref[:, lo:lo + CONV_CK])
        cg = _dot(h, win_ref[:, D + lo:D + lo + CONV_CK])
        v = _dot(h, win_ref[:, 2 * D + lo:2 * D + lo + CONV_CK])
        u = cg * v
        u_prev = jnp.where(has_prev, pltpu.roll(u, 1, axis=0), 0.0)
        u_next = jnp.where(has_next, pltpu.roll(u, rows - 1, axis=0), 0.0)
        z = (cw_ref[0:1, lo:lo + CONV_CK] * u_prev + cw_ref[1:2, lo:lo + CONV_CK] * u
             + cw_ref[2:3, lo:lo + CONV_CK] * u_next)
        bz = bg * z
        out_ref[:, lo:lo + CONV_CK] = bz[CONV_HALO:CONV_HALO + TM].astype(bf16)


def _conv_call(x, modv, layer, g, win, cw):
    hb = TM // CONV_HALO
    last = NT // CONV_HALO - 1
    return pl.pallas_call(
        _conv_kernel,
        grid=(N_TILES,),
        in_specs=[
            pl.BlockSpec((CONV_HALO, D), lambda t: (jnp.maximum(t * hb - 1, 0), 0)),
            pl.BlockSpec((TM, D), lambda t: (t, 0)),
            pl.BlockSpec((CONV_HALO, D), lambda t: (jnp.minimum((t + 1) * hb, last), 0)),
            _mod_spec(layer), _resident((1, D)), _resident((D, 3 * D)), _resident((8, D)),
        ],
        out_specs=pl.BlockSpec((TM, D), lambda t: (t, 0)),
        out_shape=jax.ShapeDtypeStruct((NT, D), bf16),
        compiler_params=pltpu.CompilerParams(vmem_limit_bytes=VMEM_LIMIT),
        name="conv_mixer",
    )(x, x, x, modv, g.reshape(1, D), win, cw)


QKV_CK = 512
NA_SCALE = 0.125


def _qkv_kernel(x_ref, mod_ref, g_ref, w_ref, out_ref):
    mod = mod_ref[...]
    h = _norm_mod(x_ref[...], g_ref[...], mod[:, 0:D], mod[:, D:2 * D]).astype(bf16)
    for c in range(3 * D // QKV_CK):
        lo = c * QKV_CK
        r = _dot(h, w_ref[:, lo:lo + QKV_CK])
        if lo < D:
            r = r * NA_SCALE
        out_ref[:, lo:lo + QKV_CK] = r.astype(bf16)


def _qkv_call(x, modv, layer, g, w):
    return pl.pallas_call(
        _qkv_kernel,
        grid=(N_TILES,),
        in_specs=[pl.BlockSpec((TM, D), lambda t: (t, 0)), _mod_spec(layer),
                  _resident((1, D)), _resident((D, 3 * D))],
        out_specs=pl.BlockSpec((TM, 3 * D), lambda t: (t, 0)),
        out_shape=jax.ShapeDtypeStruct((NT, 3 * D), bf16),
        compiler_params=pltpu.CompilerParams(vmem_limit_bytes=VMEM_LIMIT),
        name="nat_qkv",
    )(x, modv, g.reshape(1, D), w)


def _stack_pair(q2):
    lane = lax.broadcasted_iota(jnp.int32, q2.shape, 1)
    zero = jnp.zeros_like(q2)
    return jnp.concatenate([jnp.where(lane < 64, q2, zero), jnp.where(lane >= 64, q2, zero)], axis=0)


def _unstack_pair(o, n):
    lane = lax.broadcasted_iota(jnp.int32, (n, LANE), 1)
    return jnp.where(lane < 64, o[:n], o[n:])


NA_ROWS = S // GRID_W
NA_CTX_STEPS = CTX // GRID_W


def _nat_kernel(q_ref, kb_ref, vb_ref, kc_ref, vc_ref, bias_ref, out_ref):
    r = pl.program_id(1)
    lanes = [slice(hp * LANE, (hp + 1) * LANE) for hp in range(HP)]

    def attend(with_window):
        r0 = jnp.clip(r - NA_KR // 2, 0, NA_ROWS - NA_KR)
        off = pl.multiple_of(r0 * GRID_W, GRID_W)
        nband = NA_KR * GRID_W
        scores = []
        for hp, ls in enumerate(lanes):
            qst = _stack_pair(q_ref[:, ls])
            s_c = _dot_nt(qst, kc_ref[:, ls])
            s_w = _dot_nt(qst, kb_ref[pl.ds(off, nband), ls]) + bias_ref[hp] if with_window else None
            scores.append((s_w, s_c))
        probs = []
        for s_w, s_c in scores:
            m = jnp.max(s_c, axis=-1, keepdims=True)
            if with_window:
                m = jnp.maximum(m, jnp.max(s_w, axis=-1, keepdims=True))
            p_c = jnp.exp(s_c - m)
            l = jnp.sum(p_c, axis=-1, keepdims=True)
            p_w = None
            if with_window:
                p_w = jnp.exp(s_w - m)
                l = l + jnp.sum(p_w, axis=-1, keepdims=True)
                p_w = p_w.astype(bf16)
            probs.append((p_w, p_c.astype(bf16), l))
        for (p_w, p_c, l), ls in zip(probs, lanes):
            o = _dot(p_c, vc_ref[:, ls])
            if with_window:
                o = o + _dot(p_w, vb_ref[pl.ds(off, nband), ls])
            out_ref[:, ls] = _unstack_pair(o / l, GRID_W).astype(bf16)

    @pl.when(r < NA_ROWS)
    def _():
        attend(True)

    @pl.when(r >= NA_ROWS)
    def _():
        attend(False)


def _nat_call(qkv, bias):
    def q_idx(b, r):
        ctx_row = B * NA_ROWS + b * NA_CTX_STEPS + (r - NA_ROWS)
        return (jnp.where(r < NA_ROWS, b * NA_ROWS + r, ctx_row), 0)

    def bias_idx(b, r):
        rr = jnp.minimum(r, NA_ROWS - 1)
        r0 = jnp.clip(rr - NA_KR // 2, 0, NA_ROWS - NA_KR)
        return (r0 - rr + NA_KR - 1, 0, 0, 0)

    ctx_blk = T_LAT // CTX
    return pl.pallas_call(
        _nat_kernel,
        grid=(B, NA_ROWS + NA_CTX_STEPS),
        in_specs=[
            pl.BlockSpec((GRID_W, D), q_idx),
            pl.BlockSpec((S, D), lambda b, r: (b, 1)),
            pl.BlockSpec((S, D), lambda b, r: (b, 2)),
            pl.BlockSpec((CTX, D), lambda b, r: (ctx_blk + b, 1)),
            pl.BlockSpec((CTX, D), lambda b, r: (ctx_blk + b, 2)),
            pl.BlockSpec((None, HP, 2 * GRID_W, NA_KR * GRID_W), bias_idx),
        ],
        out_specs=pl.BlockSpec((GRID_W, D), q_idx),
        out_shape=jax.ShapeDtypeStruct((NT, D), bf16),
        compiler_params=pltpu.CompilerParams(vmem_limit_bytes=VMEM_LIMIT),
        name="nat_attn",
    )(qkv, qkv, qkv, qkv, qkv, bias)


def _nat_bias_table(rpb):
    cols = np.arange(GRID_W)
    c0 = np.clip(cols - NA_KC // 2, 0, GRID_W - NA_KC)
    j = cols[None, :]
    q = cols[:, None]
    valid = (j >= c0[:, None]) & (j < c0[:, None] + NA_KC)
    rel = np.clip(j - q + NA_KC - 1, 0, 2 * NA_KC - 2)
    dense = jnp.where(valid[None, None], rpb[:, :, rel], MASK_NEG)
    slabs = []
    for d0 in range(NA_KR):
        band = dense[:, d0:d0 + NA_KR]
        slabs.append(band.transpose(0, 2, 1, 3).reshape(NA_H, GRID_W, NA_KR * GRID_W))
    tab = jnp.stack(slabs, axis=0)
    return tab.reshape(NA_KR, HP, 2 * GRID_W, NA_KR * GRID_W).astype(f32)


ML_SCALE = (ML_NOPE + ML_ROPE) ** -0.5 * math.log2(math.e)
ML_QW = ML_H * LANE
ML_P1 = 2 * ML_RANK + 2 * LANE
ML_CK = 256


def _mla_proj_kernel(x_ref, mod_ref, g_ref, w1_ref, qg_ref, kvg_ref, wqa_ref, wqb_ref, wk_ref, wv_ref,
                     cos_ref, sin_ref, q_out, k_out, v_out):
    mod = mod_ref[...]
    h = _norm_mod(x_ref[...], g_ref[...], mod[:, 0:D], mod[:, D:2 * D]).astype(bf16)
    p = _dot(h, w1_ref[...])
    cq = _rmsnorm(p[:, 0:ML_RANK], qg_ref[...]).astype(bf16)
    ckv = _rmsnorm(p[:, ML_RANK:2 * ML_RANK], kvg_ref[...]).astype(bf16)
    cos = cos_ref[...]
    sin = sin_ref[...]
    krope = p[:, 2 * ML_RANK:2 * ML_RANK + LANE] * cos + p[:, 2 * ML_RANK + LANE:] * sin
    for c in range(ML_QW // ML_CK):
        lo = c * ML_CK
        qa = _dot(cq, wqa_ref[:, lo:lo + ML_CK])
        qb = _dot(cq, wqb_ref[:, lo:lo + ML_CK])
        kn = _dot(ckv, wk_ref[:, lo:lo + ML_CK])
        for e in range(ML_CK // LANE):
            ls = slice(e * LANE, (e + 1) * LANE)
            q_out[:, lo + e * LANE:lo + (e + 1) * LANE] = (
                (qa[:, ls] * cos + qb[:, ls] * sin) * ML_SCALE).astype(bf16)
            k_out[:, lo + e * LANE:lo + (e + 1) * LANE] = (kn[:, ls] + krope).astype(bf16)
    for c in range(ML_H * ML_V // ML_CK):
        lo = c * ML_CK
        v_out[:, lo:lo + ML_CK] = _dot(ckv, wv_ref[:, lo:lo + ML_CK]).astype(bf16)


def _mla_proj_call(x, modv, layer, g, w1, qg, kvg, wqa, wqb, wk, wv, cos_t, sin_t):
    rope_idx = lambda t: (jnp.where(t < N_LAT_TILES, t % TILES_PER_SEQ, TILES_PER_SEQ), 0)
    row = lambda w: pl.BlockSpec((TM, w), lambda t: (t, 0))
    return pl.pallas_call(
        _mla_proj_kernel,
        grid=(N_TILES,),
        in_specs=[row(D), _mod_spec(layer), _resident((1, D)), _resident((D, ML_P1)),
                  _resident((1, ML_RANK)), _resident((1, ML_RANK)),
                  _resident((ML_RANK, ML_QW)), _resident((ML_RANK, ML_QW)),
                  _resident((ML_RANK, ML_QW)), _resident((ML_RANK, ML_H * ML_V)),
                  pl.BlockSpec((TM, LANE), rope_idx), pl.BlockSpec((TM, LANE), rope_idx)],
        out_specs=[row(ML_QW), row(ML_QW), row(ML_H * ML_V)],
        out_shape=[jax.ShapeDtypeStruct((NT, ML_QW), bf16), jax.ShapeDtypeStruct((NT, ML_QW), bf16),
                   jax.ShapeDtypeStruct((NT, ML_H * ML_V), bf16)],
        compiler_params=pltpu.CompilerParams(vmem_limit_bytes=VMEM_LIMIT),
        name="mla_proj",
    )(x, modv, g.reshape(1, D), w1, qg.reshape(1, ML_RANK), kvg.reshape(1, ML_RANK),
      wqa, wqb, wk, wv, cos_t, sin_t)


ML_TQ = 1024
ML_TK = 1024


def _mla_attn_kernel(q_ref, kl_ref, kc_ref, vl_ref, vc_ref, out_ref):
    qs = [q_ref[:, e * LANE:(e + 1) * LANE] for e in range(2)]
    n_lat = S // ML_TK

    def keys(c):
        return kl_ref[c * ML_TK:(c + 1) * ML_TK, :] if c < n_lat else kc_ref[...]

    def values(c):
        return vl_ref[c * ML_TK:(c + 1) * ML_TK, :] if c < n_lat else vc_ref[...]

    def update(e, c, carry):
        m, l, acc = carry
        s = _dot_nt(qs[e], keys(c)[:, e * LANE:(e + 1) * LANE])
        m_new = jnp.maximum(m, jnp.max(s, axis=-1, keepdims=True))
        alpha = jnp.exp2(m - m_new)
        p = jnp.exp2(s - m_new)
        l = alpha * l + jnp.sum(p, axis=-1, keepdims=True)
        acc = alpha * acc + _dot(p.astype(bf16), values(c))
        return m_new, l, acc

    init = (jnp.full((ML_TQ, 1), -jnp.inf, f32), jnp.zeros((ML_TQ, 1), f32),
            jnp.zeros((ML_TQ, LANE), f32))
    carry = [init, init]
    for c in range(n_lat + 1):
        carry = [update(e, c, carry[e]) for e in range(2)]
    (_, l0, a0), (_, l1, a1) = carry
    lane = lax.broadcasted_iota(jnp.int32, (ML_TQ, LANE), 1)
    out_ref[...] = jnp.where(lane < 64, a0 / l0, a1 / l1).astype(bf16)


def _mla_attn_call(q, k, v):
    nq = S // ML_TQ
    ctx_blk = T_LAT // CTX
    return pl.pallas_call(
        _mla_attn_kernel,
        grid=(B, HP, nq),
        in_specs=[
            pl.BlockSpec((ML_TQ, 2 * LANE), lambda b, hp, i: (b * nq + i, hp)),
            pl.BlockSpec((S, 2 * LANE), lambda b, hp, i: (b, hp)),
            pl.BlockSpec((CTX, 2 * LANE), lambda b, hp, i: (ctx_blk + b, hp)),
            pl.BlockSpec((S, LANE), lambda b, hp, i: (b, hp)),
            pl.BlockSpec((CTX, LANE), lambda b, hp, i: (ctx_blk + b, hp)),
        ],
        out_specs=pl.BlockSpec((ML_TQ, LANE), lambda b, hp, i: (b * nq + i, hp)),
        out_shape=jax.ShapeDtypeStruct((T_LAT, D), bf16),
        compiler_params=pltpu.CompilerParams(vmem_limit_bytes=VMEM_LIMIT),
        name="mla_attn",
    )(q, k, k, v, v)


def _mla_weights(w_dq, w_uq, w_dkv, w_ukv):
    z = lambda *s: jnp.zeros(s, f32)
    uq = w_uq.reshape(ML_RANK, ML_H, ML_NOPE + ML_ROPE)
    nope, r1, r2 = uq[..., :ML_NOPE], uq[..., ML_NOPE:ML_NOPE + 16], uq[..., ML_NOPE + 16:]
    pad = z(ML_RANK, ML_H, LANE - ML_NOPE - ML_ROPE)
    wqa = jnp.concatenate([nope, r1, r2, pad], axis=-1).reshape(ML_RANK, ML_QW)
    wqb = jnp.concatenate([jnp.zeros_like(nope), -r2, r1, pad], axis=-1).reshape(ML_RANK, ML_QW)
    ukv = w_ukv.reshape(ML_RANK, ML_H, ML_NOPE + ML_V)
    wk = jnp.concatenate([ukv[..., :ML_NOPE], z(ML_RANK, ML_H, LANE - ML_NOPE)], axis=-1).reshape(ML_RANK, ML_QW)
    wv = ukv[..., ML_NOPE:].reshape(ML_RANK, ML_H * ML_V)
    wr = w_dkv[:, ML_RANK:]
    rope1 = jnp.concatenate([z(D, ML_NOPE), wr, z(D, LANE - ML_NOPE - ML_ROPE)], axis=-1)
    rope2 = jnp.concatenate([z(D, ML_NOPE), -wr[:, 16:], wr[:, :16], z(D, LANE - ML_NOPE - ML_ROPE)], axis=-1)
    w1 = jnp.concatenate([w_dq, w_dkv[:, :ML_RANK], rope1, rope2], axis=-1)
    return tuple(w.astype(bf16) for w in (w1, wqa, wqb, wk, wv))


def _rope_tables():
    n_freq = ML_ROPE // 4
    freq = 10000.0 ** (-jnp.arange(n_freq, dtype=f32) / n_freq)
    t = jnp.arange(S)
    row = (t // GRID_W).astype(f32)
    col = (t % GRID_W).astype(f32)
    ang = jnp.concatenate([row[:, None] * freq, col[:, None] * freq], axis=-1)
    cos, sin = jnp.cos(ang), jnp.sin(ang)
    pad = jnp.zeros((S, LANE - ML_NOPE - ML_ROPE), f32)
    cos_l = jnp.concatenate([jnp.ones((S, ML_NOPE), f32), cos, cos, pad], axis=-1)
    sin_l = jnp.concatenate([jnp.zeros((S, ML_NOPE), f32), sin, sin, pad], axis=-1)
    cos_c = jnp.concatenate([jnp.ones((TM, ML_NOPE + ML_ROPE), f32), pad[:TM]], axis=-1)
    return jnp.concatenate([cos_l, cos_c], axis=0), jnp.concatenate([sin_l, jnp.zeros((TM, LANE), f32)], axis=0)


FN_G = 4
FN_C = D // FN_G
FN_R = 64
FN_J = 16
assert FN_R * FN_R == S


def _scr_store(scr, row0, lane0, val):
    for t in range(val.shape[1] // LANE):
        scr[lane0 // LANE + t, row0:row0 + val.shape[0], :] = val[:, t * LANE:(t + 1) * LANE]


def _scr_rows(scr, start, size, stride):
    return jnp.concatenate([scr[t, pl.ds(start, size, stride=stride), :] for t in range(scr.shape[0])],
                           axis=1)


def _fnet_s1_kernel(x_ref, mod_ref, g_ref, cc_ref, sc_ref, la_ref, lb_ref, tr_out, ti_out,
                    a_scr, b_scr, t_scr):
    mod = mod_ref[...]
    x = x_ref[...].reshape(FN_R * FN_J, D)
    h = _norm_mod(x, g_ref[...], mod[:, 0:D], mod[:, D:2 * D]).astype(bf16)
    for gi in range(FN_G):
        ls = slice(gi * FN_C, (gi + 1) * FN_C)
        _scr_store(a_scr, 0, gi * FN_C, _dot(h[:, ls], cc_ref[...]))
        _scr_store(b_scr, 0, gi * FN_C, _dot(h[:, ls], sc_ref[...]))
    for j in range(FN_J):
        a_j = _scr_rows(a_scr, j, FN_R, FN_J).astype(bf16)
        b_j = _scr_rows(b_scr, j, FN_R, FN_J).astype(bf16)
        _scr_store(t_scr, j * 2 * FN_R, 0, _dot(la_ref[...], a_j) + _dot(lb_ref[...], b_j))
    for k in range(FN_R):
        tr_out[k] = _scr_rows(t_scr, k, FN_J, 2 * FN_R).astype(bf16)
        ti_out[k] = _scr_rows(t_scr, FN_R + k, FN_J, 2 * FN_R).astype(bf16)


def _fnet_s1_call(x4, modv, layer, g, cc, sc, la, lb):
    blk = pl.BlockSpec((None, FN_R, FN_J, D), lambda b, jb: (b, 0, jb, 0))
    t_shape = jax.ShapeDtypeStruct((B, FN_R, FN_R, D), bf16)
    return pl.pallas_call(
        _fnet_s1_kernel,
        grid=(B, FN_R // FN_J),
        in_specs=[blk, pl.BlockSpec((None, 1, N_MOD * D), lambda b, jb: (layer * 8 + b, 0, 0)),
                  _resident((1, D)), _resident((FN_C, FN_C)), _resident((FN_C, FN_C)),
                  _resident((2 * FN_R, FN_R)), _resident((2 * FN_R, FN_R))],
        out_specs=[blk, blk],
        out_shape=[t_shape, t_shape],
        scratch_shapes=[pltpu.VMEM((D // LANE, FN_R * FN_J, LANE), f32),
                        pltpu.VMEM((D // LANE, FN_R * FN_J, LANE), f32),
                        pltpu.VMEM((D // LANE, FN_J * 2 * FN_R, LANE), f32)],
        compiler_params=pltpu.CompilerParams(vmem_limit_bytes=VMEM_LIMIT),
        name="fnet_stage1",
    )(x4, modv, g.reshape(1, D), cc, sc, la, lb)


def _fnet_s2_kernel(tr_ref, ti_ref, g_ref, out_ref, y_scr):
    kb = pl.program_id(1)
    for j in range(FN_J):
        t = jnp.concatenate([tr_ref[j], ti_ref[j]], axis=0)
        _scr_store(y_scr, j * FN_R, 0, _dot(g_ref[kb * FN_J + j], t))
    scale = 1.0 / math.sqrt(S * FN_C)
    for k in range(FN_R):
        out_ref[k] = (_scr_rows(y_scr, k, FN_J, FN_R) * scale).astype(bf16)


def _fnet_s2_call(tr, ti, gtab):
    t_blk = pl.BlockSpec((None, FN_J, FN_R, D), lambda b, kb: (b, kb, 0, 0))
    return pl.pallas_call(
        _fnet_s2_kernel,
        grid=(B, FN_R // FN_J),
        in_specs=[t_blk, t_blk, _resident((FN_R, FN_R, 2 * FN_R))],
        out_specs=pl.BlockSpec((None, FN_R, FN_J, D), lambda b, kb: (b, 0, kb, 0)),
        out_shape=jax.ShapeDtypeStruct((B, FN_R, FN_R, D), bf16),
        scratch_shapes=[pltpu.VMEM((D // LANE, FN_J * FN_R, LANE), f32)],
        compiler_params=pltpu.CompilerParams(vmem_limit_bytes=VMEM_LIMIT),
        name="fnet_stage2",
    )(tr, ti, gtab)


def _cos_sin(num, den):
    ang = (np.asarray(num) % den).astype(np.float64) * (2.0 * np.pi / den)
    return np.cos(ang), np.sin(ang)


def _fnet_tables():
    ch = np.arange(FN_C)
    cc, sc = _cos_sin(ch[:, None] * ch[None, :], FN_C)
    r = np.arange(FN_R)
    c1, s1 = _cos_sin(r[:, None] * r[None, :], FN_R)
    la = np.concatenate([c1, -s1], axis=0)
    lb = np.concatenate([-s1, -c1], axis=0)
    k = r[:, None, None] + FN_R * r[None, :, None]
    gc, gs = _cos_sin(k * r[None, None, :], S)
    gtab = np.concatenate([gc, gs], axis=-1)
    return tuple(jnp.asarray(t, dtype=f32).astype(bf16) for t in (cc, sc, la, lb, gtab))


def kernel(x, c, ctx, c_ctx, mod_w, mod_b, mix_norm_g, ffn_norm_g, conv_w_in, conv_w, conv_w_out,
           nat_w_qkv, nat_rpb, nat_w_o, mla_w_dq, mla_q_norm_g, mla_w_uq, mla_w_dkv, mla_kv_norm_g,
           mla_w_ukv, mla_w_o, fnet_w_o, ffn_w_in, ffn_w_out, final_norm_g):
    cc8 = jnp.concatenate([c, c_ctx[None, :], jnp.zeros((8 - B - 1, D), f32)], axis=0)
    modv = _mod_call(cc8, mod_w, mod_b).reshape(DEPTH * 8, 1, N_MOD * D)

    w_in = ffn_w_in.astype(bf16)
    w_out = ffn_w_out.astype(bf16)
    xs = jnp.concatenate([x.reshape(T_LAT, D), ctx.reshape(T_CTX, D)], axis=0)

    cw8 = jnp.concatenate([conv_w[0], jnp.zeros((5, D), f32)], axis=0)
    bz = _conv_call(xs, modv, 0, mix_norm_g[0], conv_w_in[0].astype(bf16), cw8)
    xs = _tail_call(xs, bz, modv, 0, conv_w_out[0].astype(bf16), ffn_norm_g[0], w_in[0], w_out[0], N_TILES)

    qkv = _qkv_call(xs, modv, 1, mix_norm_g[1], nat_w_qkv[0].astype(bf16))
    o = _nat_call(qkv, _nat_bias_table(nat_rpb[0]))
    xs = _tail_call(xs, o, modv, 1, nat_w_o[0].astype(bf16), ffn_norm_g[1], w_in[1], w_out[1], N_TILES)

    w1, wqa, wqb, wk, wv = _mla_weights(mla_w_dq[0], mla_w_uq[0], mla_w_dkv[0], mla_w_ukv[0])
    cos_t, sin_t = _rope_tables()
    q, k, v = _mla_proj_call(xs, modv, 2, mix_norm_g[2], w1, mla_q_norm_g[0], mla_kv_norm_g[0],
                             wqa, wqb, wk, wv, cos_t, sin_t)
    o = _mla_attn_call(q, k, v)
    xl = _tail_call(xs, o, modv, 2, mla_w_o[0].astype(bf16), ffn_norm_g[2], w_in[2], w_out[2], N_LAT_TILES)

    cc_t, sc_t, la_t, lb_t, g_t = _fnet_tables()
    tr, ti = _fnet_s1_call(xl.reshape(B, FN_R, FN_R, D), modv, 3, mix_norm_g[3], cc_t, sc_t, la_t, lb_t)
    o = _fnet_s2_call(tr, ti, g_t).reshape(T_LAT, D)
    out = _tail_call(xl, o, modv, 3, fnet_w_o[0].astype(bf16), ffn_norm_g[3], w_in[3], w_out[3],
                     N_LAT_TILES, final_g=final_norm_g)
    return out.reshape(B, S, D)
```

```python
import functools
import math

import numpy as np
import jax
import jax.numpy as jnp
from jax import lax
from jax.experimental import pallas as pl
from jax.experimental.pallas import tpu as pltpu

D = 1024
B = 4
S = 4096
CTX = 256
DEPTH = 4
GRID_W = 64
N_MOD = 6
F = 2816
EPS = 1e-6

T_LAT = B * S
T_CTX = B * CTX
NT = T_LAT + T_CTX

NA_H = 16
NA_KR = 8
NA_KC = 16
ML_H = 16
ML_RANK = 256
ML_NOPE = 64
ML_ROPE = 32
ML_V = 64
HP = 8
LANE = 128
MASK_NEG = -1e30

TM = 512
N_TILES = NT // TM
N_LAT_TILES = T_LAT // TM
TILES_PER_SEQ = S // TM
FFN_CK = 256
VMEM_LIMIT = 56 * 1024 * 1024

f32 = jnp.float32
bf16 = jnp.bfloat16


def _resident(shape):
    nd = len(shape)
    return pl.BlockSpec(shape, lambda *_: (0,) * nd, pipeline_mode=pl.Buffered(1))


def _mod_spec(layer):
    return pl.BlockSpec((None, 1, N_MOD * D), lambda t: (layer * 8 + t // TILES_PER_SEQ, 0, 0))


def _rmsnorm(x, g):
    ms = jnp.mean(x * x, axis=-1, keepdims=True)
    return (x * lax.rsqrt(ms + EPS)) * g


def _norm_mod(x, g, shift, scale):
    return _rmsnorm(x, g) * (1.0 + scale) + shift


def _dot(a, b):
    return jnp.dot(a, b, preferred_element_type=f32)


def _dot_nt(a, b):
    return lax.dot_general(a, b, (((1,), (1,)), ((), ())), preferred_element_type=f32)


def _mod_kernel(cc_ref, w_ref, b_ref, out_ref):
    cc = cc_ref[...]
    s = (cc * jax.nn.sigmoid(cc)).astype(bf16)
    out_ref[...] = _dot(s, w_ref[...].astype(bf16)) + b_ref[...]


def _mod_call(cc8, mod_w, mod_b):
    tn = 1536
    return pl.pallas_call(
        _mod_kernel,
        grid=(DEPTH, N_MOD * D // tn),
        in_specs=[
            pl.BlockSpec((8, D), lambda l, j: (0, 0)),
            pl.BlockSpec((None, D, tn), lambda l, j: (l, 0, j)),
            pl.BlockSpec((None, 1, tn), lambda l, j: (l, 0, j)),
        ],
        out_specs=pl.BlockSpec((None, 8, tn), lambda l, j: (l, 0, j)),
        out_shape=jax.ShapeDtypeStruct((DEPTH, 8, N_MOD * D), f32),
        compiler_params=pltpu.CompilerParams(vmem_limit_bytes=VMEM_LIMIT),
        name="mod_vectors",
    )(cc8, mod_w, mod_b.reshape(DEPTH, 1, N_MOD * D))


def _tail_kernel(x_ref, o_ref, mod_ref, wo_ref, g_ref, win_ref, wout_ref, *rest, final):
    if final:
        fg_ref, out_ref, acc_ref = rest
    else:
        out_ref, acc_ref = rest
    mod = mod_ref[...]
    g1 = mod[:, 2 * D:3 * D]
    sh2 = mod[:, 3 * D:4 * D]
    sc2 = mod[:, 4 * D:5 * D]
    g2 = mod[:, 5 * D:6 * D]
    x1 = x_ref[...] + g1 * _dot(o_ref[...], wo_ref[...])
    h = _norm_mod(x1, g_ref[...], sh2, sc2).astype(bf16)
    for c in range(F // FFN_CK):
        lo = c * FFN_CK
        gate = _dot(h, win_ref[:, lo:lo + FFN_CK])
        up = _dot(h, win_ref[:, F + lo:F + lo + FFN_CK])
        a = (gate * jax.nn.sigmoid(gate) * up).astype(bf16)
        y = _dot(a, wout_ref[lo:lo + FFN_CK, :])
        if c == 0:
            acc_ref[...] = y
        else:
            acc_ref[...] += y
    x2 = x1 + g2 * acc_ref[...]
    if final:
        x2 = _rmsnorm(x2, fg_ref[...])
    out_ref[...] = x2


def _tail_call(x, o, modv, layer, wo, g, win, wout, n_tiles, final_g=None):
    final = final_g is not None
    row = lambda w: pl.BlockSpec((TM, w), lambda t: (t, 0))
    slab = lambda r, c: pl.BlockSpec((None, r, c), lambda t: (layer, 0, 0), pipeline_mode=pl.Buffered(1))
    in_specs = [row(D), row(D), _mod_spec(layer), _resident((D, D)), _resident((1, D)),
                slab(D, 2 * F), slab(F, D)]
    args = [x, o, modv, wo, g.reshape(1, D), win, wout]
    if final:
        in_specs.append(_resident((1, D)))
        args.append(final_g.reshape(1, D))
    return pl.pallas_call(
        functools.partial(_tail_kernel, final=final),
        grid=(n_tiles,),
        in_specs=in_specs,
        out_specs=row(D),
        out_shape=jax.ShapeDtypeStruct((n_tiles * TM, D), f32),
        scratch_shapes=[pltpu.VMEM((TM, D), f32)],
        compiler_params=pltpu.CompilerParams(vmem_limit_bytes=VMEM_LIMIT),
        name=f"tail_l{layer}",
    )(*args)


CONV_HALO = 16
CONV_CK = 256


def _conv_kernel(xp_ref, x_ref, xn_ref, mod_ref, g_ref, win_ref, cw_ref, out_ref):
    t = pl.program_id(0)
    mod = mod_ref[...]
    sh1 = mod[:, 0:D]
    sc1 = mod[:, D:2 * D]
    g = g_ref[...]
    h = jnp.concatenate(
        [_norm_mod(r[...], g, sh1, sc1).astype(bf16) for r in (xp_ref, x_ref, xn_ref)], axis=0)
    rows = TM + 2 * CONV_HALO
    seq_len = jnp.where(t < N_LAT_TILES, S, CTX)
    grow = t * TM - CONV_HALO + lax.broadcasted_iota(jnp.int32, (rows, CONV_CK), 0)
    pos = jnp.bitwise_and(grow, seq_len - 1)
    has_prev = pos != 0
    has_next = pos != seq_len - 1
    for j in range(D // CONV_CK):
        lo = j * CONV_CK
        bg = _dot(h, win_ref[:, lo:lo + CONV_CK])
        cg = _dot(h, win_ref[:, D + lo:D + lo + CONV_CK])
        v = _dot(h, win_ref[:, 2 * D + lo:2 * D + lo + CONV_CK])
        u = cg * v
        u_prev = jnp.where(has_prev, pltpu.roll(u, 1, axis=0), 0.0)
        u_next = jnp.where(has_next, pltpu.roll(u, rows - 1, axis=0), 0.0)
        z = (cw_ref[0:1, lo:lo + CONV_CK] * u_prev + cw_ref[1:2, lo:lo + CONV_CK] * u
             + cw_ref[2:3, lo:lo + CONV_CK] * u_next)
        bz = bg * z
        out_ref[:, lo:lo + CONV_CK] = bz[CONV_HALO:CONV_HALO + TM].astype(bf16)


def _conv_call(x, modv, layer, g, win, cw):
    hb = TM // CONV_HALO
    last = NT // CONV_HALO - 1
    return pl.pallas_call(
        _conv_kernel,
        grid=(N_TILES,),
        in_specs=[
            pl.BlockSpec((CONV_HALO, D), lambda t: (jnp.maximum(t * hb - 1, 0), 0)),
            pl.BlockSpec((TM, D), lambda t: (t, 0)),
            pl.BlockSpec((CONV_HALO, D), lambda t: (jnp.minimum((t + 1) * hb, last), 0)),
            _mod_spec(layer), _resident((1, D)), _resident((D, 3 * D)), _resident((8, D)),
        ],
        out_specs=pl.BlockSpec((TM, D), lambda t: (t, 0)),
        out_shape=jax.ShapeDtypeStruct((NT, D), bf16),
        compiler_params=pltpu.CompilerParams(vmem_limit_bytes=VMEM_LIMIT),
        name="conv_mixer",
    )(x, x, x, modv, g.reshape(1, D), win, cw)


QKV_CK = 512
NA_SCALE = 0.125


def _qkv_kernel(x_ref, mod_ref, g_ref, w_ref, out_ref):
    mod = mod_ref[...]
    h = _norm_mod(x_ref[...], g_ref[...], mod[:, 0:D], mod[:, D:2 * D]).astype(bf16)
    for c in range(3 * D // QKV_CK):
        lo = c * QKV_CK
        r = _dot(h, w_ref[:, lo:lo + QKV_CK])
        if lo < D:
            r = r * NA_SCALE
        out_ref[:, lo:lo + QKV_CK] = r.astype(bf16)


def _qkv_call(x, modv, layer, g, w):
    return pl.pallas_call(
        _qkv_kernel,
        grid=(N_TILES,),
        in_specs=[pl.BlockSpec((TM, D), lambda t: (t, 0)), _mod_spec(layer),
                  _resident((1, D)), _resident((D, 3 * D))],
        out_specs=pl.BlockSpec((TM, 3 * D), lambda t: (t, 0)),
        out_shape=jax.ShapeDtypeStruct((NT, 3 * D), bf16),
        compiler_params=pltpu.CompilerParams(vmem_limit_bytes=VMEM_LIMIT),
        name="nat_qkv",
    )(x, modv, g.reshape(1, D), w)


def _stack_pair(q2):
    lane = lax.broadcasted_iota(jnp.int32, q2.shape, 1)
    zero = jnp.zeros_like(q2)
    return jnp.concatenate([jnp.where(lane < 64, q2, zero), jnp.where(lane >= 64, q2, zero)], axis=0)


def _unstack_pair(o, n):
    lane = lax.broadcasted_iota(jnp.int32, (n, LANE), 1)
    return jnp.where(lane < 64, o[:n], o[n:])


NA_ROWS = S // GRID_W
NA_CTX_STEPS = CTX // GRID_W


def _nat_kernel(q_ref, kb_ref, vb_ref, kc_ref, vc_ref, bias_ref, out_ref):
    r = pl.program_id(1)
    lanes = [slice(hp * LANE, (hp + 1) * LANE) for hp in range(HP)]

    def attend(with_window):
        r0 = jnp.clip(r - NA_KR // 2, 0, NA_ROWS - NA_KR)
        off = pl.multiple_of(r0 * GRID_W, GRID_W)
        nband = NA_KR * GRID_W
        def gather_keys(band_ref, ctx_ref, ls):
            if not with_window:
                return ctx_ref[:, ls]
            return jnp.concatenate([band_ref[pl.ds(off, nband), ls], ctx_ref[:, ls]], axis=0)

        scores = []
        for hp, ls in enumerate(lanes):
            s = _dot_nt(_stack_pair(q_ref[:, ls]), gather_keys(kb_ref, kc_ref, ls))
            if with_window:
                s = jnp.concatenate([s[:, :nband] + bias_ref[hp], s[:, nband:]], axis=1)
            scores.append(s)
        probs = []
        for s in scores:
            p = jnp.exp(s - jnp.max(s, axis=-1, keepdims=True))
            probs.append((p.astype(bf16), jnp.sum(p, axis=-1, keepdims=True)))
        for (p, l), ls in zip(probs, lanes):
            o = _dot(p, gather_keys(vb_ref, vc_ref, ls))
            out_ref[:, ls] = _unstack_pair(o / l, GRID_W).astype(bf16)

    @pl.when(r < NA_ROWS)
    def _():
        attend(True)

    @pl.when(r >= NA_ROWS)
    def _():
        attend(False)


def _nat_call(qkv, bias):
    def q_idx(b, r):
        ctx_row = B * NA_ROWS + b * NA_CTX_STEPS + (r - NA_ROWS)
        return (jnp.where(r < NA_ROWS, b * NA_ROWS + r, ctx_row), 0)

    def bias_idx(b, r):
        rr = jnp.minimum(r, NA_ROWS - 1)
        r0 = jnp.clip(rr - NA_KR // 2, 0, NA_ROWS - NA_KR)
        return (r0 - rr + NA_KR - 1, 0, 0, 0)

    ctx_blk = T_LAT // CTX
    return pl.pallas_call(
        _nat_kernel,
        grid=(B, NA_ROWS + NA_CTX_STEPS),
        in_specs=[
            pl.BlockSpec((GRID_W, D), q_idx),
            pl.BlockSpec((S, D), lambda b, r: (b, 1)),
            pl.BlockSpec((S, D), lambda b, r: (b, 2)),
            pl.BlockSpec((CTX, D), lambda b, r: (ctx_blk + b, 1)),
            pl.BlockSpec((CTX, D), lambda b, r: (ctx_blk + b, 2)),
            pl.BlockSpec((None, HP, 2 * GRID_W, NA_KR * GRID_W), bias_idx),
        ],
        out_specs=pl.BlockSpec((GRID_W, D), q_idx),
        out_shape=jax.ShapeDtypeStruct((NT, D), bf16),
        compiler_params=pltpu.CompilerParams(vmem_limit_bytes=VMEM_LIMIT),
        name="nat_attn",
    )(qkv, qkv, qkv, qkv, qkv, bias)


def _nat_bias_table(rpb):
    cols = np.arange(GRID_W)
    c0 = np.clip(cols - NA_KC // 2, 0, GRID_W - NA_KC)
    j = cols[None, :]
    q = cols[:, None]
    valid = (j >= c0[:, None]) & (j < c0[:, None] + NA_KC)
    rel = np.clip(j - q + NA_KC - 1, 0, 2 * NA_KC - 2)
    dense = jnp.where(valid[None, None], rpb[:, :, rel], MASK_NEG)
    slabs = []
    for d0 in range(NA_KR):
        band = dense[:, d0:d0 + NA_KR]
        slabs.append(band.transpose(0, 2, 1, 3).reshape(NA_H, GRID_W, NA_KR * GRID_W))
    tab = jnp.stack(slabs, axis=0)
    return tab.reshape(NA_KR, HP, 2 * GRID_W, NA_KR * GRID_W).astype(f32)


ML_SCALE = (ML_NOPE + ML_ROPE) ** -0.5 * math.log2(math.e)
ML_QW = ML_H * LANE
ML_P1 = 2 * ML_RANK + 2 * LANE
ML_CK = 256


def _mla_proj_kernel(x_ref, mod_ref, g_ref, w1_ref, qg_ref, kvg_ref, wqa_ref, wqb_ref, wk_ref, wv_ref,
                     cos_ref, sin_ref, q_out, k_out, v_out):
    mod = mod_ref[...]
    h = _norm_mod(x_ref[...], g_ref[...], mod[:, 0:D], mod[:, D:2 * D]).astype(bf16)
    p = _dot(h, w1_ref[...])
    cq = _rmsnorm(p[:, 0:ML_RANK], qg_ref[...]).astype(bf16)
    ckv = _rmsnorm(p[:, ML_RANK:2 * ML_RANK], kvg_ref[...]).astype(bf16)
    cos = cos_ref[...]
    sin = sin_ref[...]
    krope = p[:, 2 * ML_RANK:2 * ML_RANK + LANE] * cos + p[:, 2 * ML_RANK + LANE:] * sin
    for c in range(ML_QW // ML_CK):
        lo = c * ML_CK
        qa = _dot(cq, wqa_ref[:, lo:lo + ML_CK])
        qb = _dot(cq, wqb_ref[:, lo:lo + ML_CK])
        kn = _dot(ckv, wk_ref[:, lo:lo + ML_CK])
        for e in range(ML_CK // LANE):
            ls = slice(e * LANE, (e + 1) * LANE)
            q_out[:, lo + e * LANE:lo + (e + 1) * LANE] = (
                (qa[:, ls] * cos + qb[:, ls] * sin) * ML_SCALE).astype(bf16)
            k_out[:, lo + e * LANE:lo + (e + 1) * LANE] = (kn[:, ls] + krope).astype(bf16)
    ones_lane = lax.broadcasted_iota(jnp.int32, (TM, ML_CK), 1) % LANE >= ML_V
    for c in range(ML_QW // ML_CK):
        lo = c * ML_CK
        vv = _dot(ckv, wv_ref[:, lo:lo + ML_CK])
        v_out[:, lo:lo + ML_CK] = jnp.where(ones_lane, 1.0, vv).astype(bf16)


def _mla_proj_call(x, modv, layer, g, w1, qg, kvg, wqa, wqb, wk, wv, cos_t, sin_t):
    rope_idx = lambda t: (jnp.where(t < N_LAT_TILES, t % TILES_PER_SEQ, TILES_PER_SEQ), 0)
    row = lambda w: pl.BlockSpec((TM, w), lambda t: (t, 0))
    return pl.pallas_call(
        _mla_proj_kernel,
        grid=(N_TILES,),
        in_specs=[row(D), _mod_spec(layer), _resident((1, D)), _resident((D, ML_P1)),
                  _resident((1, ML_RANK)), _resident((1, ML_RANK)),
                  _resident((ML_RANK, ML_QW)), _resident((ML_RANK, ML_QW)),
                  _resident((ML_RANK, ML_QW)), _resident((ML_RANK, ML_QW)),
                  pl.BlockSpec((TM, LANE), rope_idx), pl.BlockSpec((TM, LANE), rope_idx)],
        out_specs=[row(ML_QW), row(ML_QW), row(ML_QW)],
        out_shape=[jax.ShapeDtypeStruct((NT, ML_QW), bf16)] * 3,
        compiler_params=pltpu.CompilerParams(vmem_limit_bytes=VMEM_LIMIT),
        name="mla_proj",
    )(x, modv, g.reshape(1, D), w1, qg.reshape(1, ML_RANK), kvg.reshape(1, ML_RANK),
      wqa, wqb, wk, wv, cos_t, sin_t)


ML_TQ = 1024
ML_TK = 1024


def _mla_attn_kernel(q_ref, kl_ref, kc_ref, vl_ref, vc_ref, out_ref):
    qs = [q_ref[:, e * LANE:(e + 1) * LANE] for e in range(2)]
    n_lat = S // ML_TK

    def chunk(lat_ref, ctx_ref, c):
        lat = lat_ref[c * ML_TK:(c + 1) * ML_TK, :]
        return lat if c < n_lat - 1 else jnp.concatenate([lat, ctx_ref[...]], axis=0)

    def keys(c):
        return chunk(kl_ref, kc_ref, c)

    def values(c):
        return chunk(vl_ref, vc_ref, c)

    def update(e, c, carry):
        m, acc = carry
        ls = slice(e * LANE, (e + 1) * LANE)
        s = _dot_nt(qs[e], keys(c)[:, ls])
        m_new = jnp.maximum(m, jnp.max(s, axis=-1, keepdims=True))
        alpha = jnp.exp2(m - m_new)
        p = jnp.exp2(s - m_new)
        acc = alpha * acc + _dot(p.astype(bf16), values(c)[:, ls])
        return m_new, acc

    init = (jnp.full((ML_TQ, 1), -jnp.inf, f32), jnp.zeros((ML_TQ, LANE), f32))
    carry = [init, init]
    for c in range(n_lat):
        carry = [update(e, c, carry[e]) for e in range(2)]
    a0, a1 = carry[0][1], carry[1][1]
    lane = lax.broadcasted_iota(jnp.int32, (ML_TQ, LANE), 1)
    num = jnp.where(lane < ML_V, a0, pltpu.roll(a1, ML_V, axis=1))
    den = jnp.where(lane < ML_V, pltpu.roll(a0, ML_V, axis=1), a1)
    out_ref[...] = (num / den).astype(bf16)


def _mla_attn_call(q, k, v):
    nq = S // ML_TQ
    ctx_blk = T_LAT // CTX
    return pl.pallas_call(
        _mla_attn_kernel,
        grid=(B, HP, nq),
        in_specs=[
            pl.BlockSpec((ML_TQ, 2 * LANE), lambda b, hp, i: (b * nq + i, hp)),
            pl.BlockSpec((S, 2 * LANE), lambda b, hp, i: (b, hp)),
            pl.BlockSpec((CTX, 2 * LANE), lambda b, hp, i: (ctx_blk + b, hp)),
            pl.BlockSpec((S, 2 * LANE), lambda b, hp, i: (b, hp)),
            pl.BlockSpec((CTX, 2 * LANE), lambda b, hp, i: (ctx_blk + b, hp)),
        ],
        out_specs=pl.BlockSpec((ML_TQ, LANE), lambda b, hp, i: (b * nq + i, hp)),
        out_shape=jax.ShapeDtypeStruct((T_LAT, D), bf16),
        compiler_params=pltpu.CompilerParams(vmem_limit_bytes=VMEM_LIMIT),
        name="mla_attn",
    )(q, k, k, v, v)


def _mla_weights(w_dq, w_uq, w_dkv, w_ukv):
    z = lambda *s: jnp.zeros(s, f32)
    uq = w_uq.reshape(ML_RANK, ML_H, ML_NOPE + ML_ROPE)
    nope, r1, r2 = uq[..., :ML_NOPE], uq[..., ML_NOPE:ML_NOPE + 16], uq[..., ML_NOPE + 16:]
    pad = z(ML_RANK, ML_H, LANE - ML_NOPE - ML_ROPE)
    wqa = jnp.concatenate([nope, r1, r2, pad], axis=-1).reshape(ML_RANK, ML_QW)
    wqb = jnp.concatenate([jnp.zeros_like(nope), -r2, r1, pad], axis=-1).reshape(ML_RANK, ML_QW)
    ukv = w_ukv.reshape(ML_RANK, ML_H, ML_NOPE + ML_V)
    wk = jnp.concatenate([ukv[..., :ML_NOPE], z(ML_RANK, ML_H, LANE - ML_NOPE)], axis=-1).reshape(ML_RANK, ML_QW)
    wv = jnp.concatenate([ukv[..., ML_NOPE:], z(ML_RANK, ML_H, LANE - ML_V)], axis=-1).reshape(ML_RANK, ML_QW)
    wr = w_dkv[:, ML_RANK:]
    rope1 = jnp.concatenate([z(D, ML_NOPE), wr, z(D, LANE - ML_NOPE - ML_ROPE)], axis=-1)
    rope2 = jnp.concatenate([z(D, ML_NOPE), -wr[:, 16:], wr[:, :16], z(D, LANE - ML_NOPE - ML_ROPE)], axis=-1)
    w1 = jnp.concatenate([w_dq, w_dkv[:, :ML_RANK], rope1, rope2], axis=-1)
    return tuple(w.astype(bf16) for w in (w1, wqa, wqb, wk, wv))


def _rope_tables():
    n_freq = ML_ROPE // 4
    freq = 10000.0 ** (-jnp.arange(n_freq, dtype=f32) / n_freq)
    t = jnp.arange(S)
    row = (t // GRID_W).astype(f32)
    col = (t % GRID_W).astype(f32)
    ang = jnp.concatenate([row[:, None] * freq, col[:, None] * freq], axis=-1)
    cos, sin = jnp.cos(ang), jnp.sin(ang)
    pad = jnp.zeros((S, LANE - ML_NOPE - ML_ROPE), f32)
    cos_l = jnp.concatenate([jnp.ones((S, ML_NOPE), f32), cos, cos, pad], axis=-1)
    sin_l = jnp.concatenate([jnp.zeros((S, ML_NOPE), f32), sin, sin, pad], axis=-1)
    cos_c = jnp.concatenate([jnp.ones((TM, ML_NOPE + ML_ROPE), f32), pad[:TM]], axis=-1)
    return jnp.concatenate([cos_l, cos_c], axis=0), jnp.concatenate([sin_l, jnp.zeros((TM, LANE), f32)], axis=0)


FN_G = 4
FN_C = D // FN_G
FN_R = 64
FN_J = 16
assert FN_R * FN_R == S


def _scr_store(scr, row0, lane0, val):
    for t in range(val.shape[1] // LANE):
        scr[lane0 // LANE + t, row0:row0 + val.shape[0], :] = val[:, t * LANE:(t + 1) * LANE]


def _scr_rows(scr, start, size, stride):
    return jnp.concatenate([scr[t, pl.ds(start, size, stride=stride), :] for t in range(scr.shape[0])],
                           axis=1)


def _fnet_s1_kernel(x_ref, mod_ref, g_ref, cc_ref, sc_ref, la_ref, lb_ref, tr_out, ti_out,
                    a_scr, b_scr, t_scr):
    mod = mod_ref[...]
    x = x_ref[...].reshape(FN_R * FN_J, D)
    h = _norm_mod(x, g_ref[...], mod[:, 0:D], mod[:, D:2 * D]).astype(bf16)
    for gi in range(FN_G):
        ls = slice(gi * FN_C, (gi + 1) * FN_C)
        _scr_store(a_scr, 0, gi * FN_C, _dot(h[:, ls], cc_ref[...]))
        _scr_store(b_scr, 0, gi * FN_C, _dot(h[:, ls], sc_ref[...]))
    for j in range(FN_J):
        a_j = _scr_rows(a_scr, j, FN_R, FN_J).astype(bf16)
        b_j = _scr_rows(b_scr, j, FN_R, FN_J).astype(bf16)
        _scr_store(t_scr, j * 2 * FN_R, 0, _dot(la_ref[...], a_j) + _dot(lb_ref[...], b_j))
    for k in range(FN_R):
        tr_out[k] = _scr_rows(t_scr, k, FN_J, 2 * FN_R).astype(bf16)
        ti_out[k] = _scr_rows(t_scr, FN_R + k, FN_J, 2 * FN_R).astype(bf16)


def _fnet_s1_call(x4, modv, layer, g, cc, sc, la, lb):
    blk = pl.BlockSpec((None, FN_R, FN_J, D), lambda b, jb: (b, 0, jb, 0))
    t_shape = jax.ShapeDtypeStruct((B, FN_R, FN_R, D), bf16)
    return pl.pallas_call(
        _fnet_s1_kernel,
        grid=(B, FN_R // FN_J),
        in_specs=[blk, pl.BlockSpec((None, 1, N_MOD * D), lambda b, jb: (layer * 8 + b, 0, 0)),
                  _resident((1, D)), _resident((FN_C, FN_C)), _resident((FN_C, FN_C)),
                  _resident((2 * FN_R, FN_R)), _resident((2 * FN_R, FN_R))],
        out_specs=[blk, blk],
        out_shape=[t_shape, t_shape],
        scratch_shapes=[pltpu.VMEM((D // LANE, FN_R * FN_J, LANE), f32),
                        pltpu.VMEM((D // LANE, FN_R * FN_J, LANE), f32),
                        pltpu.VMEM((D // LANE, FN_J * 2 * FN_R, LANE), f32)],
        compiler_params=pltpu.CompilerParams(vmem_limit_bytes=VMEM_LIMIT),
        name="fnet_stage1",
    )(x4, modv, g.reshape(1, D), cc, sc, la, lb)


def _fnet_s2_kernel(tr_ref, ti_ref, g_ref, out_ref, y_scr):
    kb = pl.program_id(1)
    for j in range(FN_J):
        t = jnp.concatenate([tr_ref[j], ti_ref[j]], axis=0)
        _scr_store(y_scr, j * FN_R, 0, _dot(g_ref[kb * FN_J + j], t))
    scale = 1.0 / math.sqrt(S * FN_C)
    for k in range(FN_R):
        out_ref[k] = (_scr_rows(y_scr, k, FN_J, FN_R) * scale).astype(bf16)


def _fnet_s2_call(tr, ti, gtab):
    t_blk = pl.BlockSpec((None, FN_J, FN_R, D), lambda b, kb: (b, kb, 0, 0))
    return pl.pallas_call(
        _fnet_s2_kernel,
        grid=(B, FN_R // FN_J),
        in_specs=[t_blk, t_blk, _resident((FN_R, FN_R, 2 * FN_R))],
        out_specs=pl.BlockSpec((None, FN_R, FN_J, D), lambda b, kb: (b, 0, kb, 0)),
        out_shape=jax.ShapeDtypeStruct((B, FN_R, FN_R, D), bf16),
        scratch_shapes=[pltpu.VMEM((D // LANE, FN_J * FN_R, LANE), f32)],
        compiler_params=pltpu.CompilerParams(vmem_limit_bytes=VMEM_LIMIT),
        name="fnet_stage2",
    )(tr, ti, gtab)


def _cos_sin(num, den):
    ang = (np.asarray(num) % den).astype(np.float64) * (2.0 * np.pi / den)
    return np.cos(ang), np.sin(ang)


def _fnet_tables():
    ch = np.arange(FN_C)
    cc, sc = _cos_sin(ch[:, None] * ch[None, :], FN_C)
    r = np.arange(FN_R)
    c1, s1 = _cos_sin(r[:, None] * r[None, :], FN_R)
    la = np.concatenate([c1, -s1], axis=0)
    lb = np.concatenate([-s1, -c1], axis=0)
    k = r[:, None, None] + FN_R * r[None, :, None]
    gc, gs = _cos_sin(k * r[None, None, :], S)
    gtab = np.concatenate([gc, gs], axis=-1)
    return tuple(jnp.asarray(t, dtype=f32).astype(bf16) for t in (cc, sc, la, lb, gtab))


def kernel(x, c, ctx, c_ctx, mod_w, mod_b, mix_norm_g, ffn_norm_g, conv_w_in, conv_w, conv_w_out,
           nat_w_qkv, nat_rpb, nat_w_o, mla_w_dq, mla_q_norm_g, mla_w_uq, mla_w_dkv, mla_kv_norm_g,
           mla_w_ukv, mla_w_o, fnet_w_o, ffn_w_in, ffn_w_out, final_norm_g):
    cc8 = jnp.concatenate([c, c_ctx[None, :], jnp.zeros((8 - B - 1, D), f32)], axis=0)
    modv = _mod_call(cc8, mod_w, mod_b).reshape(DEPTH * 8, 1, N_MOD * D)

    w_in = ffn_w_in.astype(bf16)
    w_out = ffn_w_out.astype(bf16)
    xs = jnp.concatenate([x.reshape(T_LAT, D), ctx.reshape(T_CTX, D)], axis=0)

    cw8 = jnp.concatenate([conv_w[0], jnp.zeros((5, D), f32)], axis=0)
    bz = _conv_call(xs, modv, 0, mix_norm_g[0], conv_w_in[0].astype(bf16), cw8)
    xs = _tail_call(xs, bz, modv, 0, conv_w_out[0].astype(bf16), ffn_norm_g[0], w_in, w_out, N_TILES)

    qkv = _qkv_call(xs, modv, 1, mix_norm_g[1], nat_w_qkv[0].astype(bf16))
    o = _nat_call(qkv, _nat_bias_table(nat_rpb[0]))
    xs = _tail_call(xs, o, modv, 1, nat_w_o[0].astype(bf16), ffn_norm_g[1], w_in, w_out, N_TILES)

    w1, wqa, wqb, wk, wv = _mla_weights(mla_w_dq[0], mla_w_uq[0], mla_w_dkv[0], mla_w_ukv[0])
    cos_t, sin_t = _rope_tables()
    q, k, v = _mla_proj_call(xs, modv, 2, mix_norm_g[2], w1, mla_q_norm_g[0], mla_kv_norm_g[0],
                             wqa, wqb, wk, wv, cos_t, sin_t)
    o = _mla_attn_call(q, k, v)
    xl = _tail_call(xs, o, modv, 2, mla_w_o[0].astype(bf16), ffn_norm_g[2], w_in, w_out, N_LAT_TILES)

    cc_t, sc_t, la_t, lb_t, g_t = _fnet_tables()
    tr, ti = _fnet_s1_call(xl.reshape(B, FN_R, FN_R, D), modv, 3, mix_norm_g[3], cc_t, sc_t, la_t, lb_t)
    o = _fnet_s2_call(tr, ti, g_t).reshape(T_LAT, D)
    out = _tail_call(xl, o, modv, 3, fnet_w_o[0].astype(bf16), ffn_norm_g[3], w_in, w_out,
                     N_LAT_TILES, final_g=final_norm_g)
    return out.reshape(B, S, D)
```

```python
import functools
import math

import numpy as np
import jax
import jax.numpy as jnp
from jax import lax
from jax.experimental import pallas as pl
from jax.experimental.pallas import tpu as pltpu

D = 1024
B = 4
S = 4096
CTX = 256
DEPTH = 4
GRID_W = 64
N_MOD = 6
F = 2816
EPS = 1e-6

T_LAT = B * S
T_CTX = B * CTX
NT = T_LAT + T_CTX

NA_H = 16
NA_KR = 8
NA_KC = 16
ML_H = 16
ML_RANK = 256
ML_NOPE = 64
ML_ROPE = 32
ML_V = 64
HP = 8
LANE = 128
MASK_NEG = -1e30

TM = 512
N_TILES = NT // TM
N_LAT_TILES = T_LAT // TM
TILES_PER_SEQ = S // TM
FFN_CK = 256
VMEM_LIMIT = 56 * 1024 * 1024

f32 = jnp.float32
bf16 = jnp.bfloat16


def _resident(shape):
    nd = len(shape)
    return pl.BlockSpec(shape, lambda *_: (0,) * nd, pipeline_mode=pl.Buffered(1))


def _mod_spec(layer):
    return pl.BlockSpec((None, 1, N_MOD * D), lambda t: (layer * 8 + t // TILES_PER_SEQ, 0, 0))


def _rmsnorm(x, g):
    ms = jnp.mean(x * x, axis=-1, keepdims=True)
    return (x * lax.rsqrt(ms + EPS)) * g


def _norm_mod(x, g, shift, scale):
    return _rmsnorm(x, g) * (1.0 + scale) + shift


def _dot(a, b):
    return jnp.dot(a, b, preferred_element_type=f32)


def _dot_nt(a, b):
    return lax.dot_general(a, b, (((1,), (1,)), ((), ())), preferred_element_type=f32)


def _mod_kernel(cc_ref, w_ref, b_ref, out_ref):
    cc = cc_ref[...]
    s = (cc * jax.nn.sigmoid(cc)).astype(bf16)
    out_ref[...] = _dot(s, w_ref[...].astype(bf16)) + b_ref[...]


def _mod_call(cc8, mod_w, mod_b):
    tn = 1536
    return pl.pallas_call(
        _mod_kernel,
        grid=(DEPTH, N_MOD * D // tn),
        in_specs=[
            pl.BlockSpec((8, D), lambda l, j: (0, 0)),
            pl.BlockSpec((None, D, tn), lambda l, j: (l, 0, j)),
            pl.BlockSpec((None, 1, tn), lambda l, j: (l, 0, j)),
        ],
        out_specs=pl.BlockSpec((None, 8, tn), lambda l, j: (l, 0, j)),
        out_shape=jax.ShapeDtypeStruct((DEPTH, 8, N_MOD * D), f32),
        compiler_params=pltpu.CompilerParams(vmem_limit_bytes=VMEM_LIMIT),
        name="mod_vectors",
    )(cc8, mod_w, mod_b.reshape(DEPTH, 1, N_MOD * D))


def _tile_rows(lat_ref, ctx_ref):
    return jnp.where(pl.program_id(0) < N_LAT_TILES, lat_ref[...], ctx_ref[...])


def _split_specs(block_rows, lat_idx, ctx_idx):
    tl = lambda t: jnp.minimum(t, N_LAT_TILES - 1)
    tc = lambda t: jnp.maximum(t - N_LAT_TILES, 0)
    return [pl.BlockSpec((block_rows, D), lambda t: (lat_idx(tl(t)), 0)),
            pl.BlockSpec((block_rows, D), lambda t: (ctx_idx(tc(t)), 0))]


def _tail_kernel(*refs, final, split):
    if split:
        xl_ref, xc_ref, *refs = refs
        x = _tile_rows(xl_ref, xc_ref)
    else:
        x_ref, *refs = refs
        x = x_ref[...]
    o_ref, mod_ref, wo_ref, g_ref, win_ref, wout_ref, *rest = refs
    if final:
        fg_ref, out_ref, acc_ref = rest
    else:
        out_ref, acc_ref = rest
    mod = mod_ref[...]
    g1 = mod[:, 2 * D:3 * D]
    sh2 = mod[:, 3 * D:4 * D]
    sc2 = mod[:, 4 * D:5 * D]
    g2 = mod[:, 5 * D:6 * D]
    x1 = x + g1 * _dot(o_ref[...], wo_ref[...])
    h = _norm_mod(x1, g_ref[...], sh2, sc2).astype(bf16)
    for c in range(F // FFN_CK):
        lo = c * FFN_CK
        gate = _dot(h, win_ref[:, lo:lo + FFN_CK])
        up = _dot(h, win_ref[:, F + lo:F + lo + FFN_CK])
        a = (gate * jax.nn.sigmoid(gate) * up).astype(bf16)
        y = _dot(a, wout_ref[lo:lo + FFN_CK, :])
        if c == 0:
            acc_ref[...] = y
        else:
            acc_ref[...] += y
    x2 = x1 + g2 * acc_ref[...]
    if final:
        x2 = _rmsnorm(x2, fg_ref[...])
    out_ref[...] = x2


def _tail_call(x, o, modv, layer, wo, g, win, wout, n_tiles, final_g=None):
    final = final_g is not None
    split = isinstance(x, tuple)
    row = lambda w: pl.BlockSpec((TM, w), lambda t: (t, 0))
    slab = lambda r, c: pl.BlockSpec((None, r, c), lambda t: (layer, 0, 0), pipeline_mode=pl.Buffered(1))
    x_specs = _split_specs(TM, lambda i: i, lambda i: i) if split else [row(D)]
    in_specs = x_specs + [row(D), _mod_spec(layer), _resident((D, D)), _resident((1, D)),
                          slab(D, 2 * F), slab(F, D)]
    args = list(x if split else (x,)) + [o, modv, wo, g.reshape(1, D), win, wout]
    if final:
        in_specs.append(_resident((1, D)))
        args.append(final_g.reshape(1, D))
    return pl.pallas_call(
        functools.partial(_tail_kernel, final=final, split=split),
        grid=(n_tiles,),
        in_specs=in_specs,
        out_specs=row(D),
        out_shape=jax.ShapeDtypeStruct((n_tiles * TM, D), f32),
        scratch_shapes=[pltpu.VMEM((TM, D), f32)],
        compiler_params=pltpu.CompilerParams(vmem_limit_bytes=VMEM_LIMIT),
        name=f"tail_l{layer}",
    )(*args)


CONV_HALO = 16
CONV_CK = 256


def _conv_kernel(lp_ref, cp_ref, lx_ref, cx_ref, ln_ref, cn_ref, mod_ref, g_ref, win_ref, cw_ref, out_ref):
    t = pl.program_id(0)
    mod = mod_ref[...]
    sh1 = mod[:, 0:D]
    sc1 = mod[:, D:2 * D]
    g = g_ref[...]
    pieces = [_tile_rows(lp_ref, cp_ref), _tile_rows(lx_ref, cx_ref), _tile_rows(ln_ref, cn_ref)]
    h = jnp.concatenate([_norm_mod(p, g, sh1, sc1).astype(bf16) for p in pieces], axis=0)
    rows = TM + 2 * CONV_HALO
    seq_len = jnp.where(t < N_LAT_TILES, S, CTX)
    grow = t * TM - CONV_HALO + lax.broadcasted_iota(jnp.int32, (rows, CONV_CK), 0)
    pos = jnp.bitwise_and(grow, seq_len - 1)
    has_prev = pos != 0
    has_next = pos != seq_len - 1
    for j in range(D // CONV_CK):
        lo = j * CONV_CK
        bg = _dot(h, win_ref[:, lo:lo + CONV_CK])
        cg = _dot(h, win_ref[:, D + lo:D + lo + CONV_CK])
        v = _dot(h, win_ref[:, 2 * D + lo:2 * D + lo + CONV_CK])
        u = cg * v
        u_prev = jnp.where(has_prev, pltpu.roll(u, 1, axis=0), 0.0)
        u_next = jnp.where(has_next, pltpu.roll(u, rows - 1, axis=0), 0.0)
        z = (cw_ref[0:1, lo:lo + CONV_CK] * u_prev + cw_ref[1:2, lo:lo + CONV_CK] * u
             + cw_ref[2:3, lo:lo + CONV_CK] * u_next)
        bz = bg * z
        out_ref[:, lo:lo + CONV_CK] = bz[CONV_HALO:CONV_HALO + TM].astype(bf16)


def _conv_call(x_lat, x_ctx, modv, layer, g, win, cw):
    hb = TM // CONV_HALO
    prev_idx = lambda i: jnp.maximum(i * hb - 1, 0)
    next_idx = lambda rows: (lambda i: jnp.minimum((i + 1) * hb, rows // CONV_HALO - 1))
    prev_l, prev_c = _split_specs(CONV_HALO, prev_idx, prev_idx)
    next_l, next_c = _split_specs(CONV_HALO, next_idx(T_LAT), next_idx(T_CTX))
    main_l, main_c = _split_specs(TM, lambda i: i, lambda i: i)
    return pl.pallas_call(
        _conv_kernel,
        grid=(N_TILES,),
        in_specs=[prev_l, prev_c, main_l, main_c, next_l, next_c,
                  _mod_spec(layer), _resident((1, D)), _resident((D, 3 * D)), _resident((8, D))],
        out_specs=pl.BlockSpec((TM, D), lambda t: (t, 0)),
        out_shape=jax.ShapeDtypeStruct((NT, D), bf16),
        compiler_params=pltpu.CompilerParams(vmem_limit_bytes=VMEM_LIMIT),
        name="conv_mixer",
    )(x_lat, x_ctx, x_lat, x_ctx, x_lat, x_ctx, modv, g.reshape(1, D), win, cw)


QKV_CK = 512
NA_SCALE = 0.125


def _qkv_kernel(x_ref, mod_ref, g_ref, w_ref, out_ref):
    mod = mod_ref[...]
    h = _norm_mod(x_ref[...], g_ref[...], mod[:, 0:D], mod[:, D:2 * D]).astype(bf16)
    for c in range(3 * D // QKV_CK):
        lo = c * QKV_CK
        r = _dot(h, w_ref[:, lo:lo + QKV_CK])
        if lo < D:
            r = r * NA_SCALE
        out_ref[:, lo:lo + QKV_CK] = r.astype(bf16)


def _qkv_call(x, modv, layer, g, w):
    return pl.pallas_call(
        _qkv_kernel,
        grid=(N_TILES,),
        in_specs=[pl.BlockSpec((TM, D), lambda t: (t, 0)), _mod_spec(layer),
                  _resident((1, D)), _resident((D, 3 * D))],
        out_specs=pl.BlockSpec((TM, 3 * D), lambda t: (t, 0)),
        out_shape=jax.ShapeDtypeStruct((NT, 3 * D), bf16),
        compiler_params=pltpu.CompilerParams(vmem_limit_bytes=VMEM_LIMIT),
        name="nat_qkv",
    )(x, modv, g.reshape(1, D), w)


def _stack_pair(q2):
    lane = lax.broadcasted_iota(jnp.int32, q2.shape, 1)
    zero = jnp.zeros_like(q2)
    return jnp.concatenate([jnp.where(lane < 64, q2, zero), jnp.where(lane >= 64, q2, zero)], axis=0)


def _unstack_pair(o, n):
    lane = lax.broadcasted_iota(jnp.int32, (n, LANE), 1)
    return jnp.where(lane < 64, o[:n], o[n:])


NA_ROWS = S // GRID_W
NA_CTX_STEPS = CTX // GRID_W


def _nat_kernel(q_ref, kb_ref, vb_ref, kc_ref, vc_ref, bias_ref, out_ref):
    r = pl.program_id(1)
    lanes = [slice(hp * LANE, (hp + 1) * LANE) for hp in range(HP)]

    def attend(with_window):
        r0 = jnp.clip(r - NA_KR // 2, 0, NA_ROWS - NA_KR)
        off = pl.multiple_of(r0 * GRID_W, GRID_W)
        nband = NA_KR * GRID_W
        scores = []
        for hp, ls in enumerate(lanes):
            qst = _stack_pair(q_ref[:, ls])
            s_c = _dot_nt(qst, kc_ref[:, ls])
            s_w = _dot_nt(qst, kb_ref[pl.ds(off, nband), ls]) + bias_ref[hp] if with_window else None
            scores.append((s_w, s_c))
        probs = []
        for s_w, s_c in scores:
            m = jnp.max(s_c, axis=-1, keepdims=True)
            if with_window:
                m = jnp.maximum(m, jnp.max(s_w, axis=-1, keepdims=True))
            p_c = jnp.exp(s_c - m)
            l = jnp.sum(p_c, axis=-1, keepdims=True)
            p_w = None
            if with_window:
                p_w = jnp.exp(s_w - m)
                l = l + jnp.sum(p_w, axis=-1, keepdims=True)
                p_w = p_w.astype(bf16)
            probs.append((p_w, p_c.astype(bf16), l))
        for (p_w, p_c, l), ls in zip(probs, lanes):
            o = _dot(p_c, vc_ref[:, ls])
            if with_window:
                o = o + _dot(p_w, vb_ref[pl.ds(off, nband), ls])
            out_ref[:, ls] = _unstack_pair(o / l, GRID_W).astype(bf16)

    @pl.when(r < NA_ROWS)
    def _():
        attend(True)

    @pl.when(r >= NA_ROWS)
    def _():
        attend(False)


def _nat_call(qkv, bias):
    def q_idx(b, r):
        ctx_row = B * NA_ROWS + b * NA_CTX_STEPS + (r - NA_ROWS)
        return (jnp.where(r < NA_ROWS, b * NA_ROWS + r, ctx_row), 0)

    def bias_idx(b, r):
        rr = jnp.minimum(r, NA_ROWS - 1)
        r0 = jnp.clip(rr - NA_KR // 2, 0, NA_ROWS - NA_KR)
        return (r0 - rr + NA_KR - 1, 0, 0, 0)

    ctx_blk = T_LAT // CTX
    return pl.pallas_call(
        _nat_kernel,
        grid=(B, NA_ROWS + NA_CTX_STEPS),
        in_specs=[
            pl.BlockSpec((GRID_W, D), q_idx),
            pl.BlockSpec((S, D), lambda b, r: (b, 1)),
            pl.BlockSpec((S, D), lambda b, r: (b, 2)),
            pl.BlockSpec((CTX, D), lambda b, r: (ctx_blk + b, 1)),
            pl.BlockSpec((CTX, D), lambda b, r: (ctx_blk + b, 2)),
            pl.BlockSpec((None, HP, 2 * GRID_W, NA_KR * GRID_W), bias_idx),
        ],
        out_specs=pl.BlockSpec((GRID_W, D), q_idx),
        out_shape=jax.ShapeDtypeStruct((NT, D), bf16),
        compiler_params=pltpu.CompilerParams(vmem_limit_bytes=VMEM_LIMIT),
        name="nat_attn",
    )(qkv, qkv, qkv, qkv, qkv, bias)


def _nat_bias_table(rpb):
    cols = np.arange(GRID_W)
    c0 = np.clip(cols - NA_KC // 2, 0, GRID_W - NA_KC)
    j = cols[None, :]
    q = cols[:, None]
    valid = (j >= c0[:, None]) & (j < c0[:, None] + NA_KC)
    rel = np.clip(j - q + NA_KC - 1, 0, 2 * NA_KC - 2)
    dense = jnp.where(valid[None, None], rpb[:, :, rel], MASK_NEG)
    d_idx = np.arange(NA_KR)[:, None] + np.arange(NA_KR)[None, :]
    tab = dense[:, d_idx].transpose(1, 0, 3, 2, 4)
    return tab.reshape(NA_KR, HP, 2 * GRID_W, NA_KR * GRID_W).astype(f32)


ML_SCALE = (ML_NOPE + ML_ROPE) ** -0.5 * math.log2(math.e)
ML_QW = ML_H * LANE
ML_P1 = 2 * ML_RANK + 2 * LANE
ML_CK = 256


def _mla_proj_kernel(x_ref, mod_ref, g_ref, w1_ref, qg_ref, kvg_ref, wqa_ref, wqb_ref, wk_ref, wv_ref,
                     cos_ref, sin_ref, q_out, k_out, v_out):
    mod = mod_ref[...]
    h = _norm_mod(x_ref[...], g_ref[...], mod[:, 0:D], mod[:, D:2 * D]).astype(bf16)
    p = _dot(h, w1_ref[...])
    cq = _rmsnorm(p[:, 0:ML_RANK], qg_ref[...]).astype(bf16)
    ckv = _rmsnorm(p[:, ML_RANK:2 * ML_RANK], kvg_ref[...]).astype(bf16)
    cos = cos_ref[...]
    sin = sin_ref[...]
    krope = p[:, 2 * ML_RANK:2 * ML_RANK + LANE] * cos + p[:, 2 * ML_RANK + LANE:] * sin
    for c in range(ML_QW // ML_CK):
        lo = c * ML_CK
        qa = _dot(cq, wqa_ref[:, lo:lo + ML_CK])
        qb = _dot(cq, wqb_ref[:, lo:lo + ML_CK])
        kn = _dot(ckv, wk_ref[:, lo:lo + ML_CK])
        for e in range(ML_CK // LANE):
            ls = slice(e * LANE, (e + 1) * LANE)
            q_out[:, lo + e * LANE:lo + (e + 1) * LANE] = (
                (qa[:, ls] * cos + qb[:, ls] * sin) * ML_SCALE).astype(bf16)
            k_out[:, lo + e * LANE:lo + (e + 1) * LANE] = (kn[:, ls] + krope).astype(bf16)
    ones_lane = lax.broadcasted_iota(jnp.int32, (TM, ML_CK), 1) % LANE >= ML_V
    for c in range(ML_QW // ML_CK):
        lo = c * ML_CK
        vv = _dot(ckv, wv_ref[:, lo:lo + ML_CK])
        v_out[:, lo:lo + ML_CK] = jnp.where(ones_lane, 1.0, vv).astype(bf16)


def _mla_proj_call(x, modv, layer, g, w1, qg, kvg, wqa, wqb, wk, wv, cos_t, sin_t):
    rope_idx = lambda t: (jnp.where(t < N_LAT_TILES, t % TILES_PER_SEQ, TILES_PER_SEQ), 0)
    row = lambda w: pl.BlockSpec((TM, w), lambda t: (t, 0))
    return pl.pallas_call(
        _mla_proj_kernel,
        grid=(N_TILES,),
        in_specs=[row(D), _mod_spec(layer), _resident((1, D)), _resident((D, ML_P1)),
                  _resident((1, ML_RANK)), _resident((1, ML_RANK)),
                  _resident((ML_RANK, ML_QW)), _resident((ML_RANK, ML_QW)),
                  _resident((ML_RANK, ML_QW)), _resident((ML_RANK, ML_QW)),
                  pl.BlockSpec((TM, LANE), rope_idx), pl.BlockSpec((TM, LANE), rope_idx)],
        out_specs=[row(ML_QW), row(ML_QW), row(ML_QW)],
        out_shape=[jax.ShapeDtypeStruct((NT, ML_QW), bf16)] * 3,
        compiler_params=pltpu.CompilerParams(vmem_limit_bytes=VMEM_LIMIT),
        name="mla_proj",
    )(x, modv, g.reshape(1, D), w1, qg.reshape(1, ML_RANK), kvg.reshape(1, ML_RANK),
      wqa, wqb, wk, wv, cos_t, sin_t)


ML_TQ = 1024
ML_TK = 1024


def _mla_attn_kernel(q_ref, kl_ref, kc_ref, vl_ref, vc_ref, out_ref):
    qs = [q_ref[:, e * LANE:(e + 1) * LANE] for e in range(2)]
    n_lat = S // ML_TK

    def chunk(lat_ref, ctx_ref, c):
        lat = lat_ref[c * ML_TK:(c + 1) * ML_TK, :]
        return lat if c < n_lat - 1 else jnp.concatenate([lat, ctx_ref[...]], axis=0)

    def keys(c):
        return chunk(kl_ref, kc_ref, c)

    def values(c):
        return chunk(vl_ref, vc_ref, c)

    def update(e, c, carry):
        m, acc = carry
        ls = slice(e * LANE, (e + 1) * LANE)
        s = _dot_nt(qs[e], keys(c)[:, ls])
        m_new = jnp.maximum(m, jnp.max(s, axis=-1, keepdims=True))
        alpha = jnp.exp2(m - m_new)
        p = jnp.exp2(s - m_new)
        acc = alpha * acc + _dot(p.astype(bf16), values(c)[:, ls])
        return m_new, acc

    init = (jnp.full((ML_TQ, 1), -jnp.inf, f32), jnp.zeros((ML_TQ, LANE), f32))
    carry = [init, init]
    for c in range(n_lat):
        carry = [update(e, c, carry[e]) for e in range(2)]
    a0, a1 = carry[0][1], carry[1][1]
    lane = lax.broadcasted_iota(jnp.int32, (ML_TQ, LANE), 1)
    num = jnp.where(lane < ML_V, a0, pltpu.roll(a1, ML_V, axis=1))
    den = jnp.where(lane < ML_V, pltpu.roll(a0, ML_V, axis=1), a1)
    out_ref[...] = (num / den).astype(bf16)


def _mla_attn_call(q, k, v):
    nq = S // ML_TQ
    ctx_blk = T_LAT // CTX
    return pl.pallas_call(
        _mla_attn_kernel,
        grid=(B, HP, nq),
        in_specs=[
            pl.BlockSpec((ML_TQ, 2 * LANE), lambda b, hp, i: (b * nq + i, hp)),
            pl.BlockSpec((S, 2 * LANE), lambda b, hp, i: (b, hp)),
            pl.BlockSpec((CTX, 2 * LANE), lambda b, hp, i: (ctx_blk + b, hp)),
            pl.BlockSpec((S, 2 * LANE), lambda b, hp, i: (b, hp)),
            pl.BlockSpec((CTX, 2 * LANE), lambda b, hp, i: (ctx_blk + b, hp)),
        ],
        out_specs=pl.BlockSpec((ML_TQ, LANE), lambda b, hp, i: (b * nq + i, hp)),
        out_shape=jax.ShapeDtypeStruct((T_LAT, D), bf16),
        compiler_params=pltpu.CompilerParams(vmem_limit_bytes=VMEM_LIMIT),
        name="mla_attn",
    )(q, k, k, v, v)


def _mla_weights(w_dq, w_uq, w_dkv, w_ukv):
    z = lambda *s: jnp.zeros(s, f32)
    uq = w_uq.reshape(ML_RANK, ML_H, ML_NOPE + ML_ROPE)
    nope, r1, r2 = uq[..., :ML_NOPE], uq[..., ML_NOPE:ML_NOPE + 16], uq[..., ML_NOPE + 16:]
    pad = z(ML_RANK, ML_H, LANE - ML_NOPE - ML_ROPE)
    wqa = jnp.concatenate([nope, r1, r2, pad], axis=-1).reshape(ML_RANK, ML_QW)
    wqb = jnp.concatenate([jnp.zeros_like(nope), -r2, r1, pad], axis=-1).reshape(ML_RANK, ML_QW)
    ukv = w_ukv.reshape(ML_RANK, ML_H, ML_NOPE + ML_V)
    wk = jnp.concatenate([ukv[..., :ML_NOPE], z(ML_RANK, ML_H, LANE - ML_NOPE)], axis=-1).reshape(ML_RANK, ML_QW)
    wv = jnp.concatenate([ukv[..., ML_NOPE:], z(ML_RANK, ML_H, LANE - ML_V)], axis=-1).reshape(ML_RANK, ML_QW)
    wr = w_dkv[:, ML_RANK:]
    rope1 = jnp.concatenate([z(D, ML_NOPE), wr, z(D, LANE - ML_NOPE - ML_ROPE)], axis=-1)
    rope2 = jnp.concatenate([z(D, ML_NOPE), -wr[:, 16:], wr[:, :16], z(D, LANE - ML_NOPE - ML_ROPE)], axis=-1)
    w1 = jnp.concatenate([w_dq, w_dkv[:, :ML_RANK], rope1, rope2], axis=-1)
    return tuple(w.astype(bf16) for w in (w1, wqa, wqb, wk, wv))


def _rope_tables():
    n_freq = ML_ROPE // 4
    freq = 10000.0 ** (-jnp.arange(n_freq, dtype=f32) / n_freq)
    t = jnp.arange(S)
    row = (t // GRID_W).astype(f32)
    col = (t % GRID_W).astype(f32)
    ang = jnp.concatenate([row[:, None] * freq, col[:, None] * freq], axis=-1)
    cos, sin = jnp.cos(ang), jnp.sin(ang)
    pad = jnp.zeros((S, LANE - ML_NOPE - ML_ROPE), f32)
    cos_l = jnp.concatenate([jnp.ones((S, ML_NOPE), f32), cos, cos, pad], axis=-1)
    sin_l = jnp.concatenate([jnp.zeros((S, ML_NOPE), f32), sin, sin, pad], axis=-1)
    cos_c = jnp.concatenate([jnp.ones((TM, ML_NOPE + ML_ROPE), f32), pad[:TM]], axis=-1)
    return jnp.concatenate([cos_l, cos_c], axis=0), jnp.concatenate([sin_l, jnp.zeros((TM, LANE), f32)], axis=0)


FN_G = 4
FN_C = D // FN_G
FN_R = 64
FN_J = 16
assert FN_R * FN_R == S


def _regroup(x, outer, inner):
    return x.reshape(outer, inner, x.shape[-1]).swapaxes(0, 1).reshape(x.shape)


def _fnet_s1_kernel(x_ref, mod_ref, g_ref, l1_ref, tr_out, ti_out):
    mod = mod_ref[...]
    x = x_ref[...].reshape(FN_R * FN_J, D)
    h = _norm_mod(x, g_ref[...], mod[:, 0:D], mod[:, D:2 * D])
    h = _regroup(h, FN_R, FN_J).astype(bf16)
    t = jnp.concatenate([_dot(l1_ref[...], h[j * FN_R:(j + 1) * FN_R]) for j in range(FN_J)], axis=0)
    t = _regroup(t, FN_J, 2 * FN_R)
    half = FN_R * FN_J
    tr_out[...] = t[:half].astype(bf16).reshape(FN_R, FN_J, D)
    ti_out[...] = t[half:].astype(bf16).reshape(FN_R, FN_J, D)


def _fnet_s1_call(x4, modv, layer, g, l1):
    blk = pl.BlockSpec((None, FN_R, FN_J, D), lambda b, jb: (b, 0, jb, 0))
    t_shape = jax.ShapeDtypeStruct((B, FN_R, FN_R, D), bf16)
    return pl.pallas_call(
        _fnet_s1_kernel,
        grid=(B, FN_R // FN_J),
        in_specs=[blk, pl.BlockSpec((None, 1, N_MOD * D), lambda b, jb: (layer * 8 + b, 0, 0)),
                  _resident((1, D)), _resident((2 * FN_R, FN_R))],
        out_specs=[blk, blk],
        out_shape=[t_shape, t_shape],
        compiler_params=pltpu.CompilerParams(vmem_limit_bytes=VMEM_LIMIT),
        name="fnet_stage1",
    )(x4, modv, g.reshape(1, D), l1)


def _fnet_s2_kernel(tr_ref, ti_ref, wc_ref, g_ref, out_ref):
    kb = pl.program_id(1)
    tr = tr_ref[...].reshape(FN_J * FN_R, D)
    ti = ti_ref[...].reshape(FN_J * FN_R, D)
    ur, ui = [], []
    for gi in range(FN_G):
        ls = slice(gi * FN_C, (gi + 1) * FN_C)
        u = _dot(jnp.concatenate([tr[:, ls], ti[:, ls]], axis=1), wc_ref[...])
        ur.append(u[:, :FN_C].astype(bf16))
        ui.append(u[:, FN_C:].astype(bf16))
    ur = jnp.concatenate(ur, axis=1)
    ui = jnp.concatenate(ui, axis=1)
    ys = []
    for j in range(FN_J):
        rows = slice(j * FN_R, (j + 1) * FN_R)
        t = jnp.concatenate([ur[rows], ui[rows]], axis=0)
        ys.append(_dot(g_ref[kb * FN_J + j], t))
    y = _regroup(jnp.concatenate(ys, axis=0), FN_J, FN_R)
    out_ref[...] = (y * (1.0 / math.sqrt(S * FN_C))).astype(bf16).reshape(FN_R, FN_J, D)


def _fnet_s2_call(tr, ti, wc, gtab):
    t_blk = pl.BlockSpec((None, FN_J, FN_R, D), lambda b, kb: (b, kb, 0, 0))
    return pl.pallas_call(
        _fnet_s2_kernel,
        grid=(B, FN_R // FN_J),
        in_specs=[t_blk, t_blk, _resident((2 * FN_C, 2 * FN_C)), _resident((FN_R, FN_R, 2 * FN_R))],
        out_specs=pl.BlockSpec((None, FN_R, FN_J, D), lambda b, kb: (b, 0, kb, 0)),
        out_shape=jax.ShapeDtypeStruct((B, FN_R, FN_R, D), bf16),
        compiler_params=pltpu.CompilerParams(vmem_limit_bytes=VMEM_LIMIT),
        name="fnet_stage2",
    )(tr, ti, wc, gtab)


def _cos_sin(num, den):
    ang = (np.asarray(num) % den).astype(np.float64) * (2.0 * np.pi / den)
    return np.cos(ang), np.sin(ang)


def _fnet_tables():
    ch = np.arange(FN_C)
    cc, sc = _cos_sin(ch[:, None] * ch[None, :], FN_C)
    r = np.arange(FN_R)
    c1, s1 = _cos_sin(r[:, None] * r[None, :], FN_R)
    l1 = np.concatenate([c1, -s1], axis=0)
    wc = np.block([[cc, -sc], [sc, cc]])
    k = r[:, None, None] + FN_R * r[None, :, None]
    gc, gs = _cos_sin(k * r[None, None, :], S)
    gtab = np.concatenate([gc, gs], axis=-1)
    return tuple(jnp.asarray(t, dtype=f32).astype(bf16) for t in (l1, wc, gtab))


def kernel(x, c, ctx, c_ctx, mod_w, mod_b, mix_norm_g, ffn_norm_g, conv_w_in, conv_w, conv_w_out,
           nat_w_qkv, nat_rpb, nat_w_o, mla_w_dq, mla_q_norm_g, mla_w_uq, mla_w_dkv, mla_kv_norm_g,
           mla_w_ukv, mla_w_o, fnet_w_o, ffn_w_in, ffn_w_out, final_norm_g):
    cc8 = jnp.concatenate([c, c_ctx[None, :], jnp.zeros((8 - B - 1, D), f32)], axis=0)
    modv = _mod_call(cc8, mod_w, mod_b).reshape(DEPTH * 8, 1, N_MOD * D)

    w_in = ffn_w_in.astype(bf16)
    w_out = ffn_w_out.astype(bf16)
    x_lat = x.reshape(T_LAT, D)
    x_ctx = ctx.reshape(T_CTX, D)

    cw8 = jnp.concatenate([conv_w[0], jnp.zeros((5, D), f32)], axis=0)
    bz = _conv_call(x_lat, x_ctx, modv, 0, mix_norm_g[0], conv_w_in[0].astype(bf16), cw8)
    xs = _tail_call((x_lat, x_ctx), bz, modv, 0, conv_w_out[0].astype(bf16), ffn_norm_g[0], w_in, w_out,
                    N_TILES)

    qkv = _qkv_call(xs, modv, 1, mix_norm_g[1], nat_w_qkv[0].astype(bf16))
    o = _nat_call(qkv, _nat_bias_table(nat_rpb[0]))
    xs = _tail_call(xs, o, modv, 1, nat_w_o[0].astype(bf16), ffn_norm_g[1], w_in, w_out, N_TILES)

    w1, wqa, wqb, wk, wv = _mla_weights(mla_w_dq[0], mla_w_uq[0], mla_w_dkv[0], mla_w_ukv[0])
    cos_t, sin_t = _rope_tables()
    q, k, v = _mla_proj_call(xs, modv, 2, mix_norm_g[2], w1, mla_q_norm_g[0], mla_kv_norm_g[0],
                             wqa, wqb, wk, wv, cos_t, sin_t)
    o = _mla_attn_call(q, k, v)
    xl = _tail_call(xs, o, modv, 2, mla_w_o[0].astype(bf16), ffn_norm_g[2], w_in, w_out, N_LAT_TILES)

    l1_t, wc_t, g_t = _fnet_tables()
    tr, ti = _fnet_s1_call(xl.reshape(B, FN_R, FN_R, D), modv, 3, mix_norm_g[3], l1_t)
    o = _fnet_s2_call(tr, ti, wc_t, g_t).reshape(T_LAT, D)
    out = _tail_call(xl, o, modv, 3, fnet_w_o[0].astype(bf16), ffn_norm_g[3], w_in, w_out,
                     N_LAT_TILES, final_g=final_norm_g)
    return out.reshape(B, S, D)
```

```python
import functools
import math

import numpy as np
import jax
import jax.numpy as jnp
from jax import lax
from jax.experimental import pallas as pl
from jax.experimental.pallas import tpu as pltpu

D = 1024
B = 4
S = 4096
CTX = 256
DEPTH = 4
GRID_W = 64
N_MOD = 6
F = 2816
EPS = 1e-6

T_LAT = B * S
T_CTX = B * CTX
NT = T_LAT + T_CTX

NA_H = 16
NA_KR = 8
NA_KC = 16
ML_H = 16
ML_RANK = 256
ML_NOPE = 64
ML_ROPE = 32
ML_V = 64
HP = 8
LANE = 128
MASK_NEG = -1e30

TM = 512
N_TILES = NT // TM
N_LAT_TILES = T_LAT // TM
TILES_PER_SEQ = S // TM
FFN_CK = 256
VMEM_LIMIT = 56 * 1024 * 1024

f32 = jnp.float32
bf16 = jnp.bfloat16


def _resident(shape):
    nd = len(shape)
    return pl.BlockSpec(shape, lambda *_: (0,) * nd, pipeline_mode=pl.Buffered(1))


def _mod_spec(layer):
    return pl.BlockSpec((None, 1, N_MOD * D), lambda t: (layer * 8 + t // TILES_PER_SEQ, 0, 0))


def _rmsnorm(x, g):
    ms = jnp.mean(x * x, axis=-1, keepdims=True)
    return (x * lax.rsqrt(ms + EPS)) * g


def _norm_mod(x, g, shift, scale):
    return _rmsnorm(x, g) * (1.0 + scale) + shift


def _dot(a, b):
    return jnp.dot(a, b, preferred_element_type=f32)


def _dot_nt(a, b):
    return lax.dot_general(a, b, (((1,), (1,)), ((), ())), preferred_element_type=f32)


def _mod_kernel(cc_ref, w_ref, b_ref, out_ref):
    cc = cc_ref[...]
    s = (cc * jax.nn.sigmoid(cc)).astype(bf16)
    out_ref[...] = _dot(s, w_ref[...].astype(bf16)) + b_ref[...]


def _mod_call(cc8, mod_w, mod_b):
    tn = 1536
    return pl.pallas_call(
        _mod_kernel,
        grid=(DEPTH, N_MOD * D // tn),
        in_specs=[
            pl.BlockSpec((8, D), lambda l, j: (0, 0)),
            pl.BlockSpec((None, D, tn), lambda l, j: (l, 0, j)),
            pl.BlockSpec((None, 1, tn), lambda l, j: (l, 0, j)),
        ],
        out_specs=pl.BlockSpec((None, 8, tn), lambda l, j: (l, 0, j)),
        out_shape=jax.ShapeDtypeStruct((DEPTH, 8, N_MOD * D), f32),
        compiler_params=pltpu.CompilerParams(vmem_limit_bytes=VMEM_LIMIT),
        name="mod_vectors",
    )(cc8, mod_w, mod_b.reshape(DEPTH, 1, N_MOD * D))


def _tile_rows(lat_ref, ctx_ref):
    return jnp.where(pl.program_id(0) < N_LAT_TILES, lat_ref[...], ctx_ref[...])


def _split_specs(block_rows, lat_idx, ctx_idx):
    tl = lambda t: jnp.minimum(t, N_LAT_TILES - 1)
    tc = lambda t: jnp.maximum(t - N_LAT_TILES, 0)
    return [pl.BlockSpec((block_rows, D), lambda t: (lat_idx(tl(t)), 0)),
            pl.BlockSpec((block_rows, D), lambda t: (ctx_idx(tc(t)), 0))]


def _tail_kernel(*refs, final, split):
    if split:
        xl_ref, xc_ref, *refs = refs
        x = _tile_rows(xl_ref, xc_ref)
    else:
        x_ref, *refs = refs
        x = x_ref[...]
    o_ref, mod_ref, wo_ref, g_ref, win_ref, wout_ref, *rest = refs
    if final:
        fg_ref, out_ref, acc_ref = rest
    else:
        out_ref, acc_ref = rest
    mod = mod_ref[...]
    g1 = mod[:, 2 * D:3 * D]
    sh2 = mod[:, 3 * D:4 * D]
    sc2 = mod[:, 4 * D:5 * D]
    g2 = mod[:, 5 * D:6 * D]
    x1 = x + g1 * _dot(o_ref[...], wo_ref[...])
    h = _norm_mod(x1, g_ref[...], sh2, sc2).astype(bf16)
    for c in range(F // FFN_CK):
        lo = c * FFN_CK
        gate = _dot(h, win_ref[:, lo:lo + FFN_CK])
        up = _dot(h, win_ref[:, F + lo:F + lo + FFN_CK])
        a = (gate * jax.nn.sigmoid(gate) * up).astype(bf16)
        y = _dot(a, wout_ref[lo:lo + FFN_CK, :])
        if c == 0:
            acc_ref[...] = y
        else:
            acc_ref[...] += y
    x2 = x1 + g2 * acc_ref[...]
    if final:
        x2 = _rmsnorm(x2, fg_ref[...])
    out_ref[...] = x2


def _tail_call(x, o, modv, layer, wo, g, win, wout, n_tiles, final_g=None):
    final = final_g is not None
    split = isinstance(x, tuple)
    row = lambda w: pl.BlockSpec((TM, w), lambda t: (t, 0))
    slab = lambda r, c: pl.BlockSpec((None, r, c), lambda t: (layer, 0, 0), pipeline_mode=pl.Buffered(1))
    x_specs = _split_specs(TM, lambda i: i, lambda i: i) if split else [row(D)]
    in_specs = x_specs + [row(D), _mod_spec(layer), _resident((D, D)), _resident((1, D)),
                          slab(D, 2 * F), slab(F, D)]
    args = list(x if split else (x,)) + [o, modv, wo, g.reshape(1, D), win, wout]
    if final:
        in_specs.append(_resident((1, D)))
        args.append(final_g.reshape(1, D))
    return pl.pallas_call(
        functools.partial(_tail_kernel, final=final, split=split),
        grid=(n_tiles,),
        in_specs=in_specs,
        out_specs=row(D),
        out_shape=jax.ShapeDtypeStruct((n_tiles * TM, D), f32),
        scratch_shapes=[pltpu.VMEM((TM, D), f32)],
        compiler_params=pltpu.CompilerParams(vmem_limit_bytes=VMEM_LIMIT),
        name=f"tail_l{layer}",
    )(*args)


CONV_HALO = 16
CONV_CK = 256


def _conv_kernel(lp_ref, cp_ref, lx_ref, cx_ref, ln_ref, cn_ref, mod_ref, g_ref, win_ref, cw_ref, out_ref):
    t = pl.program_id(0)
    mod = mod_ref[...]
    sh1 = mod[:, 0:D]
    sc1 = mod[:, D:2 * D]
    g = g_ref[...]
    pieces = [_tile_rows(lp_ref, cp_ref), _tile_rows(lx_ref, cx_ref), _tile_rows(ln_ref, cn_ref)]
    h = jnp.concatenate([_norm_mod(p, g, sh1, sc1).astype(bf16) for p in pieces], axis=0)
    rows = TM + 2 * CONV_HALO
    seq_len = jnp.where(t < N_LAT_TILES, S, CTX)
    grow = t * TM - CONV_HALO + lax.broadcasted_iota(jnp.int32, (rows, CONV_CK), 0)
    pos = jnp.bitwise_and(grow, seq_len - 1)
    has_prev = pos != 0
    has_next = pos != seq_len - 1
    for j in range(D // CONV_CK):
        lo = j * CONV_CK
        bg = _dot(h, win_ref[:, lo:lo + CONV_CK])
        cg = _dot(h, win_ref[:, D + lo:D + lo + CONV_CK])
        v = _dot(h, win_ref[:, 2 * D + lo:2 * D + lo + CONV_CK])
        u = cg * v
        u_prev = jnp.where(has_prev, pltpu.roll(u, 1, axis=0), 0.0)
        u_next = jnp.where(has_next, pltpu.roll(u, rows - 1, axis=0), 0.0)
        z = (cw_ref[0:1, lo:lo + CONV_CK] * u_prev + cw_ref[1:2, lo:lo + CONV_CK] * u
             + cw_ref[2:3, lo:lo + CONV_CK] * u_next)
        bz = bg * z
        out_ref[:, lo:lo + CONV_CK] = bz[CONV_HALO:CONV_HALO + TM].astype(bf16)


def _conv_call(x_lat, x_ctx, modv, layer, g, win, cw):
    hb = TM // CONV_HALO
    prev_idx = lambda i: jnp.maximum(i * hb - 1, 0)
    next_idx = lambda rows: (lambda i: jnp.minimum((i + 1) * hb, rows // CONV_HALO - 1))
    prev_l, prev_c = _split_specs(CONV_HALO, prev_idx, prev_idx)
    next_l, next_c = _split_specs(CONV_HALO, next_idx(T_LAT), next_idx(T_CTX))
    main_l, main_c = _split_specs(TM, lambda i: i, lambda i: i)
    return pl.pallas_call(
        _conv_kernel,
        grid=(N_TILES,),
        in_specs=[prev_l, prev_c, main_l, main_c, next_l, next_c,
                  _mod_spec(layer), _resident((1, D)), _resident((D, 3 * D)), _resident((8, D))],
        out_specs=pl.BlockSpec((TM, D), lambda t: (t, 0)),
        out_shape=jax.ShapeDtypeStruct((NT, D), bf16),
        compiler_params=pltpu.CompilerParams(vmem_limit_bytes=VMEM_LIMIT),
        name="conv_mixer",
    )(x_lat, x_ctx, x_lat, x_ctx, x_lat, x_ctx, modv, g.reshape(1, D), win, cw)


QKV_CK = 512
NA_SCALE = 0.125


def _qkv_kernel(x_ref, mod_ref, g_ref, w_ref, out_ref):
    mod = mod_ref[...]
    h = _norm_mod(x_ref[...], g_ref[...], mod[:, 0:D], mod[:, D:2 * D]).astype(bf16)
    for c in range(3 * D // QKV_CK):
        lo = c * QKV_CK
        r = _dot(h, w_ref[:, lo:lo + QKV_CK])
        if lo < D:
            r = r * NA_SCALE
        out_ref[:, lo:lo + QKV_CK] = r.astype(bf16)


def _qkv_call(x, modv, layer, g, w):
    return pl.pallas_call(
        _qkv_kernel,
        grid=(N_TILES,),
        in_specs=[pl.BlockSpec((TM, D), lambda t: (t, 0)), _mod_spec(layer),
                  _resident((1, D)), _resident((D, 3 * D))],
        out_specs=pl.BlockSpec((TM, 3 * D), lambda t: (t, 0)),
        out_shape=jax.ShapeDtypeStruct((NT, 3 * D), bf16),
        compiler_params=pltpu.CompilerParams(vmem_limit_bytes=VMEM_LIMIT),
        name="nat_qkv",
    )(x, modv, g.reshape(1, D), w)


def _stack_pair(q2):
    lane = lax.broadcasted_iota(jnp.int32, q2.shape, 1)
    zero = jnp.zeros_like(q2)
    return jnp.concatenate([jnp.where(lane < 64, q2, zero), jnp.where(lane >= 64, q2, zero)], axis=0)


def _unstack_pair(o, n):
    lane = lax.broadcasted_iota(jnp.int32, (n, LANE), 1)
    return jnp.where(lane < 64, o[:n], o[n:])


NA_ROWS = S // GRID_W
NA_CTX_STEPS = CTX // GRID_W


def _nat_kernel(q_ref, kb_ref, vb_ref, kc_ref, vc_ref, bias_ref, out_ref):
    r = pl.program_id(1)
    lanes = [slice(hp * LANE, (hp + 1) * LANE) for hp in range(HP)]

    def attend(with_window):
        r0 = jnp.clip(r - NA_KR // 2, 0, NA_ROWS - NA_KR)
        nband = NA_KR * GRID_W

        def band(ref, ls):
            return ref[pl.ds(r0, NA_KR), :, ls].reshape(nband, LANE)

        scores = []
        for hp, ls in enumerate(lanes):
            qst = _stack_pair(q_ref[:, ls])
            s_c = _dot_nt(qst, kc_ref[:, ls])
            s_w = _dot_nt(qst, band(kb_ref, ls)) + bias_ref[hp] if with_window else None
            scores.append((s_w, s_c))
        probs = []
        for s_w, s_c in scores:
            m = jnp.max(s_c, axis=-1, keepdims=True)
            if with_window:
                m = jnp.maximum(m, jnp.max(s_w, axis=-1, keepdims=True))
            p_c = jnp.exp(s_c - m)
            l = jnp.sum(p_c, axis=-1, keepdims=True)
            p_w = None
            if with_window:
                p_w = jnp.exp(s_w - m)
                l = l + jnp.sum(p_w, axis=-1, keepdims=True)
                p_w = p_w.astype(bf16)
            probs.append((p_w, p_c.astype(bf16), l))
        for (p_w, p_c, l), ls in zip(probs, lanes):
            o = _dot(p_c, vc_ref[:, ls])
            if with_window:
                o = o + _dot(p_w, band(vb_ref, ls))
            out_ref[:, ls] = _unstack_pair(o / l, GRID_W).astype(bf16)

    @pl.when(r < NA_ROWS)
    def _():
        attend(True)

    @pl.when(r >= NA_ROWS)
    def _():
        attend(False)


def _nat_call(qkv, bias):
    qkv_rows = qkv.reshape(NT // GRID_W, GRID_W, 3 * D)

    def q_idx(b, r):
        ctx_row = B * NA_ROWS + b * NA_CTX_STEPS + (r - NA_ROWS)
        return (jnp.where(r < NA_ROWS, b * NA_ROWS + r, ctx_row), 0)

    def bias_idx(b, r):
        rr = jnp.minimum(r, NA_ROWS - 1)
        r0 = jnp.clip(rr - NA_KR // 2, 0, NA_ROWS - NA_KR)
        return (r0 - rr + NA_KR - 1, 0, 0, 0)

    ctx_blk = T_LAT // CTX
    return pl.pallas_call(
        _nat_kernel,
        grid=(B, NA_ROWS + NA_CTX_STEPS),
        in_specs=[
            pl.BlockSpec((GRID_W, D), q_idx),
            pl.BlockSpec((NA_ROWS, GRID_W, D), lambda b, r: (b, 0, 1)),
            pl.BlockSpec((NA_ROWS, GRID_W, D), lambda b, r: (b, 0, 2)),
            pl.BlockSpec((CTX, D), lambda b, r: (ctx_blk + b, 1)),
            pl.BlockSpec((CTX, D), lambda b, r: (ctx_blk + b, 2)),
            pl.BlockSpec((None, HP, 2 * GRID_W, NA_KR * GRID_W), bias_idx),
        ],
        out_specs=pl.BlockSpec((GRID_W, D), q_idx),
        out_shape=jax.ShapeDtypeStruct((NT, D), bf16),
        compiler_params=pltpu.CompilerParams(vmem_limit_bytes=VMEM_LIMIT),
        name="nat_attn",
    )(qkv, qkv_rows, qkv_rows, qkv, qkv, bias)


def _nat_bias_kernel(rpb_ref, out_ref):
    d0 = pl.program_id(0)
    q = lax.broadcasted_iota(jnp.int32, (GRID_W, LANE), 0)
    lane = lax.broadcasted_iota(jnp.int32, (GRID_W, LANE), 1)
    j = lane % GRID_W
    c0 = jnp.clip(q - NA_KC // 2, 0, GRID_W - NA_KC)
    valid = (j >= c0) & (j < c0 + NA_KC)
    for h in range(NA_H):
        hp, e = divmod(h, 2)
        for m in range(NA_KR // 2):
            halves = []
            for u in range(2):
                row = jnp.broadcast_to(rpb_ref[h, pl.ds(d0 + 2 * m + u, 1), :], (GRID_W, LANE))
                shift = (LANE - (NA_KC - 1) + u * GRID_W) % LANE
                halves.append(pltpu.roll(row, shift, axis=1, stride=1, stride_axis=0))
            t = jnp.where(lane < GRID_W, halves[0], halves[1])
            out_ref[hp, e * GRID_W:(e + 1) * GRID_W, m * LANE:(m + 1) * LANE] = jnp.where(valid, t, MASK_NEG)


def _nat_bias_table(rpb):
    rows, taps = 2 * NA_KR - 1, 2 * NA_KC - 1
    rp = jnp.pad(rpb, ((0, 0), (0, 2 * NA_KR - rows), (0, LANE - taps)))
    return pl.pallas_call(
        _nat_bias_kernel,
        grid=(NA_KR,),
        in_specs=[pl.BlockSpec((NA_H, 2 * NA_KR, LANE), lambda d: (0, 0, 0))],
        out_specs=pl.BlockSpec((None, HP, 2 * GRID_W, NA_KR * GRID_W), lambda d: (d, 0, 0, 0)),
        out_shape=jax.ShapeDtypeStruct((NA_KR, HP, 2 * GRID_W, NA_KR * GRID_W), f32),
        compiler_params=pltpu.CompilerParams(vmem_limit_bytes=VMEM_LIMIT),
        name="nat_bias",
    )(rp)


ML_SCALE = (ML_NOPE + ML_ROPE) ** -0.5 * math.log2(math.e)
ML_QW = ML_H * LANE
ML_P1 = 2 * ML_RANK + 2 * LANE
ML_CK = 256


def _mla_proj_kernel(x_ref, mod_ref, g_ref, w1_ref, qg_ref, kvg_ref, wqa_ref, wqb_ref, wk_ref, wv_ref,
                     cos_ref, sin_ref, q_out, k_out, v_out):
    mod = mod_ref[...]
    h = _norm_mod(x_ref[...], g_ref[...], mod[:, 0:D], mod[:, D:2 * D]).astype(bf16)
    p = _dot(h, w1_ref[...])
    cq = _rmsnorm(p[:, 0:ML_RANK], qg_ref[...]).astype(bf16)
    ckv = _rmsnorm(p[:, ML_RANK:2 * ML_RANK], kvg_ref[...]).astype(bf16)
    cos = cos_ref[...]
    sin = sin_ref[...]
    krope = p[:, 2 * ML_RANK:2 * ML_RANK + LANE] * cos + p[:, 2 * ML_RANK + LANE:] * sin
    for c in range(ML_QW // ML_CK):
        lo = c * ML_CK
        qa = _dot(cq, wqa_ref[:, lo:lo + ML_CK])
        qb = _dot(cq, wqb_ref[:, lo:lo + ML_CK])
        kn = _dot(ckv, wk_ref[:, lo:lo + ML_CK])
        for e in range(ML_CK // LANE):
            ls = slice(e * LANE, (e + 1) * LANE)
            q_out[:, lo + e * LANE:lo + (e + 1) * LANE] = (
                (qa[:, ls] * cos + qb[:, ls] * sin) * ML_SCALE).astype(bf16)
            k_out[:, lo + e * LANE:lo + (e + 1) * LANE] = (kn[:, ls] + krope).astype(bf16)
    ones_lane = lax.broadcasted_iota(jnp.int32, (TM, ML_CK), 1) % LANE >= ML_V
    for c in range(ML_QW // ML_CK):
        lo = c * ML_CK
        vv = _dot(ckv, wv_ref[:, lo:lo + ML_CK])
        v_out[:, lo:lo + ML_CK] = jnp.where(ones_lane, 1.0, vv).astype(bf16)


def _mla_proj_call(x, modv, layer, g, w1, qg, kvg, wqa, wqb, wk, wv, cos_t, sin_t):
    rope_idx = lambda t: (jnp.where(t < N_LAT_TILES, t % TILES_PER_SEQ, TILES_PER_SEQ), 0)
    row = lambda w: pl.BlockSpec((TM, w), lambda t: (t, 0))
    return pl.pallas_call(
        _mla_proj_kernel,
        grid=(N_TILES,),
        in_specs=[row(D), _mod_spec(layer), _resident((1, D)), _resident((D, ML_P1)),
                  _resident((1, ML_RANK)), _resident((1, ML_RANK)),
                  _resident((ML_RANK, ML_QW)), _resident((ML_RANK, ML_QW)),
                  _resident((ML_RANK, ML_QW)), _resident((ML_RANK, ML_QW)),
                  pl.BlockSpec((TM, LANE), rope_idx), pl.BlockSpec((TM, LANE), rope_idx)],
        out_specs=[row(ML_QW), row(ML_QW), row(ML_QW)],
        out_shape=[jax.ShapeDtypeStruct((NT, ML_QW), bf16)] * 3,
        compiler_params=pltpu.CompilerParams(vmem_limit_bytes=VMEM_LIMIT),
        name="mla_proj",
    )(x, modv, g.reshape(1, D), w1, qg.reshape(1, ML_RANK), kvg.reshape(1, ML_RANK),
      wqa, wqb, wk, wv, cos_t, sin_t)


ML_TQ = 1024
ML_TK = 1024


def _mla_attn_kernel(q_ref, kl_ref, kc_ref, vl_ref, vc_ref, out_ref):
    qs = [q_ref[:, e * LANE:(e + 1) * LANE] for e in range(2)]
    n_lat = S // ML_TK

    def chunk(lat_ref, ctx_ref, c):
        lat = lat_ref[c * ML_TK:(c + 1) * ML_TK, :]
        return lat if c < n_lat - 1 else jnp.concatenate([lat, ctx_ref[...]], axis=0)

    def keys(c):
        return chunk(kl_ref, kc_ref, c)

    def values(c):
        return chunk(vl_ref, vc_ref, c)

    def update(e, c, carry):
        m, acc = carry
        ls = slice(e * LANE, (e + 1) * LANE)
        s = _dot_nt(qs[e], keys(c)[:, ls])
        m_new = jnp.maximum(m, jnp.max(s, axis=-1, keepdims=True))
        alpha = jnp.exp2(m - m_new)
        p = jnp.exp2(s - m_new)
        acc = alpha * acc + _dot(p.astype(bf16), values(c)[:, ls])
        return m_new, acc

    init = (jnp.full((ML_TQ, 1), -jnp.inf, f32), jnp.zeros((ML_TQ, LANE), f32))
    carry = [init, init]
    for c in range(n_lat):
        carry = [update(e, c, carry[e]) for e in range(2)]
    a0, a1 = carry[0][1], carry[1][1]
    lane = lax.broadcasted_iota(jnp.int32, (ML_TQ, LANE), 1)
    num = jnp.where(lane < ML_V, a0, pltpu.roll(a1, ML_V, axis=1))
    den = jnp.where(lane < ML_V, pltpu.roll(a0, ML_V, axis=1), a1)
    out_ref[...] = (num / den).astype(bf16)


def _mla_attn_call(q, k, v):
    nq = S // ML_TQ
    ctx_blk = T_LAT // CTX
    return pl.pallas_call(
        _mla_attn_kernel,
        grid=(B, HP, nq),
        in_specs=[
            pl.BlockSpec((ML_TQ, 2 * LANE), lambda b, hp, i: (b * nq + i, hp)),
            pl.BlockSpec((S, 2 * LANE), lambda b, hp, i: (b, hp)),
            pl.BlockSpec((CTX, 2 * LANE), lambda b, hp, i: (ctx_blk + b, hp)),
            pl.BlockSpec((S, 2 * LANE), lambda b, hp, i: (b, hp)),
            pl.BlockSpec((CTX, 2 * LANE), lambda b, hp, i: (ctx_blk + b, hp)),
        ],
        out_specs=pl.BlockSpec((ML_TQ, LANE), lambda b, hp, i: (b * nq + i, hp)),
        out_shape=jax.ShapeDtypeStruct((T_LAT, D), bf16),
        compiler_params=pltpu.CompilerParams(vmem_limit_bytes=VMEM_LIMIT),
        name="mla_attn",
    )(q, k, k, v, v)


def _mla_weights(w_dq, w_uq, w_dkv, w_ukv):
    z = lambda *s: jnp.zeros(s, f32)
    uq = w_uq.reshape(ML_RANK, ML_H, ML_NOPE + ML_ROPE)
    nope, r1, r2 = uq[..., :ML_NOPE], uq[..., ML_NOPE:ML_NOPE + 16], uq[..., ML_NOPE + 16:]
    pad = z(ML_RANK, ML_H, LANE - ML_NOPE - ML_ROPE)
    wqa = jnp.concatenate([nope, r1, r2, pad], axis=-1).reshape(ML_RANK, ML_QW)
    wqb = jnp.concatenate([jnp.zeros_like(nope), -r2, r1, pad], axis=-1).reshape(ML_RANK, ML_QW)
    ukv = w_ukv.reshape(ML_RANK, ML_H, ML_NOPE + ML_V)
    wk = jnp.concatenate([ukv[..., :ML_NOPE], z(ML_RANK, ML_H, LANE - ML_NOPE)], axis=-1).reshape(ML_RANK, ML_QW)
    wv = jnp.concatenate([ukv[..., ML_NOPE:], z(ML_RANK, ML_H, LANE - ML_V)], axis=-1).reshape(ML_RANK, ML_QW)
    wr = w_dkv[:, ML_RANK:]
    rope1 = jnp.concatenate([z(D, ML_NOPE), wr, z(D, LANE - ML_NOPE - ML_ROPE)], axis=-1)
    rope2 = jnp.concatenate([z(D, ML_NOPE), -wr[:, 16:], wr[:, :16], z(D, LANE - ML_NOPE - ML_ROPE)], axis=-1)
    w1 = jnp.concatenate([w_dq, w_dkv[:, :ML_RANK], rope1, rope2], axis=-1)
    return tuple(w.astype(bf16) for w in (w1, wqa, wqb, wk, wv))


def _rope_tables():
    n_freq = ML_ROPE // 4
    freq = 10000.0 ** (-jnp.arange(n_freq, dtype=f32) / n_freq)
    t = jnp.arange(S)
    row = (t // GRID_W).astype(f32)
    col = (t % GRID_W).astype(f32)
    ang = jnp.concatenate([row[:, None] * freq, col[:, None] * freq], axis=-1)
    cos, sin = jnp.cos(ang), jnp.sin(ang)
    pad = jnp.zeros((S, LANE - ML_NOPE - ML_ROPE), f32)
    cos_l = jnp.concatenate([jnp.ones((S, ML_NOPE), f32), cos, cos, pad], axis=-1)
    sin_l = jnp.concatenate([jnp.zeros((S, ML_NOPE), f32), sin, sin, pad], axis=-1)
    cos_c = jnp.concatenate([jnp.ones((TM, ML_NOPE + ML_ROPE), f32), pad[:TM]], axis=-1)
    return jnp.concatenate([cos_l, cos_c], axis=0), jnp.concatenate([sin_l, jnp.zeros((TM, LANE), f32)], axis=0)


FN_G = 4
FN_C = D // FN_G
FN_R = 64
FN_J = 16
assert FN_R * FN_R == S


def _regroup(x, outer, inner):
    return x.reshape(outer, inner, x.shape[-1]).swapaxes(0, 1).reshape(x.shape)


def _fnet_s1_kernel(x_ref, mod_ref, g_ref, l1_ref, tr_out, ti_out):
    mod = mod_ref[...]
    x = x_ref[...].reshape(FN_R * FN_J, D)
    h = _norm_mod(x, g_ref[...], mod[:, 0:D], mod[:, D:2 * D])
    h = _regroup(h, FN_R, FN_J).astype(bf16)
    t = jnp.concatenate([_dot(l1_ref[...], h[j * FN_R:(j + 1) * FN_R]) for j in range(FN_J)], axis=0)
    t = _regroup(t, FN_J, 2 * FN_R)
    half = FN_R * FN_J
    tr_out[...] = t[:half].astype(bf16).reshape(FN_R, FN_J, D)
    ti_out[...] = t[half:].astype(bf16).reshape(FN_R, FN_J, D)


def _fnet_s1_call(x4, modv, layer, g, l1):
    blk = pl.BlockSpec((None, FN_R, FN_J, D), lambda b, jb: (b, 0, jb, 0))
    t_shape = jax.ShapeDtypeStruct((B, FN_R, FN_R, D), bf16)
    return pl.pallas_call(
        _fnet_s1_kernel,
        grid=(B, FN_R // FN_J),
        in_specs=[blk, pl.BlockSpec((None, 1, N_MOD * D), lambda b, jb: (layer * 8 + b, 0, 0)),
                  _resident((1, D)), _resident((2 * FN_R, FN_R))],
        out_specs=[blk, blk],
        out_shape=[t_shape, t_shape],
        compiler_params=pltpu.CompilerParams(vmem_limit_bytes=VMEM_LIMIT),
        name="fnet_stage1",
    )(x4, modv, g.reshape(1, D), l1)


def _fnet_s2_kernel(tr_ref, ti_ref, wc_ref, g_ref, out_ref):
    kb = pl.program_id(1)
    tr = tr_ref[...].reshape(FN_J * FN_R, D)
    ti = ti_ref[...].reshape(FN_J * FN_R, D)
    ur, ui = [], []
    for gi in range(FN_G):
        ls = slice(gi * FN_C, (gi + 1) * FN_C)
        u = _dot(jnp.concatenate([tr[:, ls], ti[:, ls]], axis=1), wc_ref[...])
        ur.append(u[:, :FN_C].astype(bf16))
        ui.append(u[:, FN_C:].astype(bf16))
    ur = jnp.concatenate(ur, axis=1)
    ui = jnp.concatenate(ui, axis=1)
    ys = []
    for j in range(FN_J):
        rows = slice(j * FN_R, (j + 1) * FN_R)
        t = jnp.concatenate([ur[rows], ui[rows]], axis=0)
        ys.append(_dot(g_ref[kb * FN_J + j], t))
    y = _regroup(jnp.concatenate(ys, axis=0), FN_J, FN_R)
    out_ref[...] = (y * (1.0 / math.sqrt(S * FN_C))).astype(bf16).reshape(FN_R, FN_J, D)


def _fnet_s2_call(tr, ti, wc, gtab):
    t_blk = pl.BlockSpec((None, FN_J, FN_R, D), lambda b, kb: (b, kb, 0, 0))
    return pl.pallas_call(
        _fnet_s2_kernel,
        grid=(B, FN_R // FN_J),
        in_specs=[t_blk, t_blk, _resident((2 * FN_C, 2 * FN_C)), _resident((FN_R, FN_R, 2 * FN_R))],
        out_specs=pl.BlockSpec((None, FN_R, FN_J, D), lambda b, kb: (b, 0, kb, 0)),
        out_shape=jax.ShapeDtypeStruct((B, FN_R, FN_R, D), bf16),
        compiler_params=pltpu.CompilerParams(vmem_limit_bytes=VMEM_LIMIT),
        name="fnet_stage2",
    )(tr, ti, wc, gtab)


def _cos_sin(num, den):
    ang = (np.asarray(num) % den).astype(np.float64) * (2.0 * np.pi / den)
    return np.cos(ang), np.sin(ang)


def _fnet_tables():
    ch = np.arange(FN_C)
    cc, sc = _cos_sin(ch[:, None] * ch[None, :], FN_C)
    r = np.arange(FN_R)
    c1, s1 = _cos_sin(r[:, None] * r[None, :], FN_R)
    l1 = np.concatenate([c1, -s1], axis=0)
    wc = np.block([[cc, -sc], [sc, cc]])
    k = r[:, None, None] + FN_R * r[None, :, None]
    gc, gs = _cos_sin(k * r[None, None, :], S)
    gtab = np.concatenate([gc, gs], axis=-1)
    return tuple(jnp.asarray(t, dtype=f32).astype(bf16) for t in (l1, wc, gtab))


def kernel(x, c, ctx, c_ctx, mod_w, mod_b, mix_norm_g, ffn_norm_g, conv_w_in, conv_w, conv_w_out,
           nat_w_qkv, nat_rpb, nat_w_o, mla_w_dq, mla_q_norm_g, mla_w_uq, mla_w_dkv, mla_kv_norm_g,
           mla_w_ukv, mla_w_o, fnet_w_o, ffn_w_in, ffn_w_out, final_norm_g):
    cc8 = jnp.concatenate([c, c_ctx[None, :], jnp.zeros((8 - B - 1, D), f32)], axis=0)
    modv = _mod_call(cc8, mod_w, mod_b).reshape(DEPTH * 8, 1, N_MOD * D)

    w_in = ffn_w_in.astype(bf16)
    w_out = ffn_w_out.astype(bf16)
    x_lat = x.reshape(T_LAT, D)
    x_ctx = ctx.reshape(T_CTX, D)

    cw8 = jnp.concatenate([conv_w[0], jnp.zeros((5, D), f32)], axis=0)
    bz = _conv_call(x_lat, x_ctx, modv, 0, mix_norm_g[0], conv_w_in[0].astype(bf16), cw8)
    xs = _tail_call((x_lat, x_ctx), bz, modv, 0, conv_w_out[0].astype(bf16), ffn_norm_g[0], w_in, w_out,
                    N_TILES)

    qkv = _qkv_call(xs, modv, 1, mix_norm_g[1], nat_w_qkv[0].astype(bf16))
    o = _nat_call(qkv, _nat_bias_table(nat_rpb[0]))
    xs = _tail_call(xs, o, modv, 1, nat_w_o[0].astype(bf16), ffn_norm_g[1], w_in, w_out, N_TILES)

    w1, wqa, wqb, wk, wv = _mla_weights(mla_w_dq[0], mla_w_uq[0], mla_w_dkv[0], mla_w_ukv[0])
    cos_t, sin_t = _rope_tables()
    q, k, v = _mla_proj_call(xs, modv, 2, mix_norm_g[2], w1, mla_q_norm_g[0], mla_kv_norm_g[0],
                             wqa, wqb, wk, wv, cos_t, sin_t)
    o = _mla_attn_call(q, k, v)
    xl = _tail_call(xs, o, modv, 2, mla_w_o[0].astype(bf16), ffn_norm_g[2], w_in, w_out, N_LAT_TILES)

    l1_t, wc_t, g_t = _fnet_tables()
    tr, ti = _fnet_s1_call(xl.reshape(B, FN_R, FN_R, D), modv, 3, mix_norm_g[3], l1_t)
    o = _fnet_s2_call(tr, ti, wc_t, g_t).reshape(T_LAT, D)
    out = _tail_call(xl, o, modv, 3, fnet_w_o[0].astype(bf16), ffn_norm_g[3], w_in, w_out,
                     N_LAT_TILES, final_g=final_norm_g)
    return out.reshape(B, S, D)
```

```python
import functools
import math

import numpy as np
import jax
import jax.numpy as jnp
from jax import lax
from jax.experimental import pallas as pl
from jax.experimental.pallas import tpu as pltpu

D = 1024
B = 4
S = 4096
CTX = 256
DEPTH = 4
GRID_W = 64
N_MOD = 6
F = 2816
EPS = 1e-6

T_LAT = B * S
T_CTX = B * CTX
NT = T_LAT + T_CTX

NA_H = 16
NA_KR = 8
NA_KC = 16
ML_H = 16
ML_RANK = 256
ML_NOPE = 64
ML_ROPE = 32
ML_V = 64
HP = 8
LANE = 128
MASK_NEG = -1e30

TM = 512
N_TILES = NT // TM
N_LAT_TILES = T_LAT // TM
TILES_PER_SEQ = S // TM
FFN_CK = 256
VMEM_LIMIT = 56 * 1024 * 1024

f32 = jnp.float32
bf16 = jnp.bfloat16


def _resident(shape):
    nd = len(shape)
    return pl.BlockSpec(shape, lambda *_: (0,) * nd, pipeline_mode=pl.Buffered(1))


def _mod_spec(layer):
    return pl.BlockSpec((None, 1, N_MOD * D), lambda t: (layer * 8 + t // TILES_PER_SEQ, 0, 0))


def _rmsnorm(x, g):
    ms = jnp.mean(x * x, axis=-1, keepdims=True)
    return (x * lax.rsqrt(ms + EPS)) * g


def _norm_mod(x, g, shift, scale):
    return _rmsnorm(x, g) * (1.0 + scale) + shift


def _dot(a, b):
    return jnp.dot(a, b, preferred_element_type=f32)


def _dot_nt(a, b):
    return lax.dot_general(a, b, (((1,), (1,)), ((), ())), preferred_element_type=f32)


def _mod_kernel(cc_ref, w_ref, b_ref, out_ref):
    cc = cc_ref[...]
    s = (cc * jax.nn.sigmoid(cc)).astype(bf16)
    out_ref[...] = _dot(s, w_ref[...].astype(bf16)) + b_ref[...]


def _mod_call(cc8, mod_w, mod_b):
    tn = 1536
    return pl.pallas_call(
        _mod_kernel,
        grid=(DEPTH, N_MOD * D // tn),
        in_specs=[
            pl.BlockSpec((8, D), lambda l, j: (0, 0)),
            pl.BlockSpec((None, D, tn), lambda l, j: (l, 0, j)),
            pl.BlockSpec((None, 1, tn), lambda l, j: (l, 0, j)),
        ],
        out_specs=pl.BlockSpec((None, 8, tn), lambda l, j: (l, 0, j)),
        out_shape=jax.ShapeDtypeStruct((DEPTH, 8, N_MOD * D), f32),
        compiler_params=pltpu.CompilerParams(vmem_limit_bytes=VMEM_LIMIT),
        name="mod_vectors",
    )(cc8, mod_w, mod_b.reshape(DEPTH, 1, N_MOD * D))


def _tile_rows(lat_ref, ctx_ref):
    return jnp.where(pl.program_id(0) < N_LAT_TILES, lat_ref[...], ctx_ref[...])


def _split_specs(block_rows, lat_idx, ctx_idx):
    tl = lambda t: jnp.minimum(t, N_LAT_TILES - 1)
    tc = lambda t: jnp.maximum(t - N_LAT_TILES, 0)
    return [pl.BlockSpec((block_rows, D), lambda t: (lat_idx(tl(t)), 0)),
            pl.BlockSpec((block_rows, D), lambda t: (ctx_idx(tc(t)), 0))]


def _tail_kernel(*refs, final, split):
    if split:
        xl_ref, xc_ref, *refs = refs
        x = _tile_rows(xl_ref, xc_ref)
    else:
        x_ref, *refs = refs
        x = x_ref[...]
    o_ref, mod_ref, wo_ref, g_ref, win_ref, wout_ref, *rest = refs
    if final:
        fg_ref, out_ref, acc_ref = rest
    else:
        out_ref, acc_ref = rest
    mod = mod_ref[...]
    g1 = mod[:, 2 * D:3 * D]
    sh2 = mod[:, 3 * D:4 * D]
    sc2 = mod[:, 4 * D:5 * D]
    g2 = mod[:, 5 * D:6 * D]
    x1 = x + g1 * _dot(o_ref[...], wo_ref[...])
    h = _norm_mod(x1, g_ref[...], sh2, sc2).astype(bf16)
    for c in range(F // FFN_CK):
        lo = c * FFN_CK
        gate = _dot(h, win_ref[:, lo:lo + FFN_CK])
        up = _dot(h, win_ref[:, F + lo:F + lo + FFN_CK])
        a = (gate * jax.nn.sigmoid(gate) * up).astype(bf16)
        y = _dot(a, wout_ref[lo:lo + FFN_CK, :])
        if c == 0:
            acc_ref[...] = y
        else:
            acc_ref[...] += y
    x2 = x1 + g2 * acc_ref[...]
    if final:
        x2 = _rmsnorm(x2, fg_ref[...])
    out_ref[...] = x2


def _tail_call(x, o, modv, layer, wo, g, win, wout, n_tiles, final_g=None):
    final = final_g is not None
    split = isinstance(x, tuple)
    row = lambda w: pl.BlockSpec((TM, w), lambda t: (t, 0))
    slab = lambda r, c: pl.BlockSpec((None, r, c), lambda t: (layer, 0, 0), pipeline_mode=pl.Buffered(1))
    x_specs = _split_specs(TM, lambda i: i, lambda i: i) if split else [row(D)]
    in_specs = x_specs + [row(D), _mod_spec(layer), _resident((D, D)), _resident((1, D)),
                          slab(D, 2 * F), slab(F, D)]
    args = list(x if split else (x,)) + [o, modv, wo, g.reshape(1, D), win, wout]
    if final:
        in_specs.append(_resident((1, D)))
        args.append(final_g.reshape(1, D))
    return pl.pallas_call(
        functools.partial(_tail_kernel, final=final, split=split),
        grid=(n_tiles,),
        in_specs=in_specs,
        out_specs=row(D),
        out_shape=jax.ShapeDtypeStruct((n_tiles * TM, D), f32),
        scratch_shapes=[pltpu.VMEM((TM, D), f32)],
        compiler_params=pltpu.CompilerParams(vmem_limit_bytes=VMEM_LIMIT),
        name=f"tail_l{layer}",
    )(*args)


CONV_HALO = 16
CONV_CK = 256


def _conv_kernel(lp_ref, cp_ref, lx_ref, cx_ref, ln_ref, cn_ref, mod_ref, g_ref, win_ref, cw_ref, out_ref):
    t = pl.program_id(0)
    mod = mod_ref[...]
    sh1 = mod[:, 0:D]
    sc1 = mod[:, D:2 * D]
    g = g_ref[...]
    pieces = [_tile_rows(lp_ref, cp_ref), _tile_rows(lx_ref, cx_ref), _tile_rows(ln_ref, cn_ref)]
    h = jnp.concatenate([_norm_mod(p, g, sh1, sc1).astype(bf16) for p in pieces], axis=0)
    rows = TM + 2 * CONV_HALO
    seq_len = jnp.where(t < N_LAT_TILES, S, CTX)
    grow = t * TM - CONV_HALO + lax.broadcasted_iota(jnp.int32, (rows, CONV_CK), 0)
    pos = jnp.bitwise_and(grow, seq_len - 1)
    has_prev = pos != 0
    has_next = pos != seq_len - 1
    for j in range(D // CONV_CK):
        lo = j * CONV_CK
        bg = _dot(h, win_ref[:, lo:lo + CONV_CK])
        cg = _dot(h, win_ref[:, D + lo:D + lo + CONV_CK])
        v = _dot(h, win_ref[:, 2 * D + lo:2 * D + lo + CONV_CK])
        u = cg * v
        u_prev = jnp.where(has_prev, pltpu.roll(u, 1, axis=0), 0.0)
        u_next = jnp.where(has_next, pltpu.roll(u, rows - 1, axis=0), 0.0)
        z = (cw_ref[0:1, lo:lo + CONV_CK] * u_prev + cw_ref[1:2, lo:lo + CONV_CK] * u
             + cw_ref[2:3, lo:lo + CONV_CK] * u_next)
        bz = bg * z
        out_ref[:, lo:lo + CONV_CK] = bz[CONV_HALO:CONV_HALO + TM].astype(bf16)


def _conv_call(x_lat, x_ctx, modv, layer, g, win, cw):
    hb = TM // CONV_HALO
    prev_idx = lambda i: jnp.maximum(i * hb - 1, 0)
    next_idx = lambda rows: (lambda i: jnp.minimum((i + 1) * hb, rows // CONV_HALO - 1))
    prev_l, prev_c = _split_specs(CONV_HALO, prev_idx, prev_idx)
    next_l, next_c = _split_specs(CONV_HALO, next_idx(T_LAT), next_idx(T_CTX))
    main_l, main_c = _split_specs(TM, lambda i: i, lambda i: i)
    return pl.pallas_call(
        _conv_kernel,
        grid=(N_TILES,),
        in_specs=[prev_l, prev_c, main_l, main_c, next_l, next_c,
                  _mod_spec(layer), _resident((1, D)), _resident((D, 3 * D)), _resident((8, D))],
        out_specs=pl.BlockSpec((TM, D), lambda t: (t, 0)),
        out_shape=jax.ShapeDtypeStruct((NT, D), bf16),
        compiler_params=pltpu.CompilerParams(vmem_limit_bytes=VMEM_LIMIT),
        name="conv_mixer",
    )(x_lat, x_ctx, x_lat, x_ctx, x_lat, x_ctx, modv, g.reshape(1, D), win, cw)


QKV_CK = 512
NA_SCALE = 0.125


def _qkv_kernel(x_ref, mod_ref, g_ref, w_ref, out_ref):
    mod = mod_ref[...]
    h = _norm_mod(x_ref[...], g_ref[...], mod[:, 0:D], mod[:, D:2 * D]).astype(bf16)
    for c in range(3 * D // QKV_CK):
        lo = c * QKV_CK
        r = _dot(h, w_ref[:, lo:lo + QKV_CK])
        if lo < D:
            r = r * NA_SCALE
        out_ref[:, lo:lo + QKV_CK] = r.astype(bf16)


def _qkv_call(x, modv, layer, g, w):
    return pl.pallas_call(
        _qkv_kernel,
        grid=(N_TILES,),
        in_specs=[pl.BlockSpec((TM, D), lambda t: (t, 0)), _mod_spec(layer),
                  _resident((1, D)), _resident((D, 3 * D))],
        out_specs=pl.BlockSpec((TM, 3 * D), lambda t: (t, 0)),
        out_shape=jax.ShapeDtypeStruct((NT, 3 * D), bf16),
        compiler_params=pltpu.CompilerParams(vmem_limit_bytes=VMEM_LIMIT),
        name="nat_qkv",
    )(x, modv, g.reshape(1, D), w)


def _stack_pair(q2):
    lane = lax.broadcasted_iota(jnp.int32, q2.shape, 1)
    zero = jnp.zeros_like(q2)
    return jnp.concatenate([jnp.where(lane < 64, q2, zero), jnp.where(lane >= 64, q2, zero)], axis=0)


def _unstack_pair(o, n):
    lane = lax.broadcasted_iota(jnp.int32, (n, LANE), 1)
    return jnp.where(lane < 64, o[:n], o[n:])


NA_ROWS = S // GRID_W
NA_SUB = 4
NA_WIN_STEPS = NA_ROWS // NA_SUB
NA_CTX_STEPS = CTX // (NA_SUB * GRID_W)


def _nat_kernel(q_ref, kb_ref, vb_ref, kc_ref, vc_ref, bias_ref, out_ref):
    step = pl.program_id(1)
    lanes = [slice(hp * LANE, (hp + 1) * LANE) for hp in range(HP)]
    nband = NA_KR * GRID_W

    def attend(with_window):
        scores = []
        for sub in range(NA_SUB):
            rows = slice(sub * GRID_W, (sub + 1) * GRID_W)
            r = step * NA_SUB + sub
            r0 = jnp.clip(r - NA_KR // 2, 0, NA_ROWS - NA_KR)
            off = pl.multiple_of(r0 * GRID_W, GRID_W)
            for hp, ls in enumerate(lanes):
                qst = _stack_pair(q_ref[rows, ls])
                s_c = _dot_nt(qst, kc_ref[:, ls])
                s_w = None
                if with_window:
                    s_w = _dot_nt(qst, kb_ref[pl.ds(off, nband), ls]) + bias_ref[r0 - r + NA_KR - 1, hp]
                scores.append((s_w, s_c, off, rows, ls))
        probs = []
        for s_w, s_c, off, rows, ls in scores:
            m = jnp.max(s_c, axis=-1, keepdims=True)
            if with_window:
                m = jnp.maximum(m, jnp.max(s_w, axis=-1, keepdims=True))
            p_c = jnp.exp(s_c - m)
            l = jnp.sum(p_c, axis=-1, keepdims=True)
            p_w = None
            if with_window:
                p_w = jnp.exp(s_w - m)
                l = l + jnp.sum(p_w, axis=-1, keepdims=True)
                p_w = p_w.astype(bf16)
            probs.append((p_w, p_c.astype(bf16), l, off, rows, ls))
        for p_w, p_c, l, off, rows, ls in probs:
            o = _dot(p_c, vc_ref[:, ls])
            if with_window:
                o = o + _dot(p_w, vb_ref[pl.ds(off, nband), ls])
            out_ref[rows, ls] = _unstack_pair(o / l, GRID_W).astype(bf16)

    @pl.when(step < NA_WIN_STEPS)
    def _():
        attend(True)

    @pl.when(step >= NA_WIN_STEPS)
    def _():
        attend(False)


def _nat_call(qkv, bias):
    tq = NA_SUB * GRID_W

    def q_idx(b, s):
        ctx_blk = T_LAT // tq + b * NA_CTX_STEPS + (s - NA_WIN_STEPS)
        return (jnp.where(s < NA_WIN_STEPS, b * NA_WIN_STEPS + s, ctx_blk), 0)

    once = dict(pipeline_mode=pl.Buffered(1))
    ctx_blk = T_LAT // CTX
    return pl.pallas_call(
        _nat_kernel,
        grid=(B, NA_WIN_STEPS + NA_CTX_STEPS),
        in_specs=[
            pl.BlockSpec((tq, D), q_idx),
            pl.BlockSpec((S, D), lambda b, s: (b, 1), **once),
            pl.BlockSpec((S, D), lambda b, s: (b, 2), **once),
            pl.BlockSpec((CTX, D), lambda b, s: (ctx_blk + b, 1)),
            pl.BlockSpec((CTX, D), lambda b, s: (ctx_blk + b, 2)),
            _resident((NA_KR, HP, 2 * GRID_W, NA_KR * GRID_W)),
        ],
        out_specs=pl.BlockSpec((tq, D), q_idx),
        out_shape=jax.ShapeDtypeStruct((NT, D), bf16),
        compiler_params=pltpu.CompilerParams(vmem_limit_bytes=VMEM_LIMIT),
        name="nat_attn",
    )(qkv, qkv, qkv, qkv, qkv, bias)


def _nat_bias_kernel(rpb_ref, out_ref):
    d0 = pl.program_id(0)
    q = lax.broadcasted_iota(jnp.int32, (GRID_W, LANE), 0)
    lane = lax.broadcasted_iota(jnp.int32, (GRID_W, LANE), 1)
    j = lane % GRID_W
    c0 = jnp.clip(q - NA_KC // 2, 0, GRID_W - NA_KC)
    valid = (j >= c0) & (j < c0 + NA_KC)
    for h in range(NA_H):
        hp, e = divmod(h, 2)
        for m in range(NA_KR // 2):
            halves = []
            for u in range(2):
                row = jnp.broadcast_to(rpb_ref[h, pl.ds(d0 + 2 * m + u, 1), :], (GRID_W, LANE))
                shift = (LANE - (NA_KC - 1) + u * GRID_W) % LANE
                halves.append(pltpu.roll(row, shift, axis=1, stride=1, stride_axis=0))
            t = jnp.where(lane < GRID_W, halves[0], halves[1])
            out_ref[hp, e * GRID_W:(e + 1) * GRID_W, m * LANE:(m + 1) * LANE] = jnp.where(valid, t, MASK_NEG)


def _nat_bias_table(rpb):
    rows, taps = 2 * NA_KR - 1, 2 * NA_KC - 1
    rp = jnp.pad(rpb, ((0, 0), (0, 2 * NA_KR - rows), (0, LANE - taps)))
    return pl.pallas_call(
        _nat_bias_kernel,
        grid=(NA_KR,),
        in_specs=[pl.BlockSpec((NA_H, 2 * NA_KR, LANE), lambda d: (0, 0, 0))],
        out_specs=pl.BlockSpec((None, HP, 2 * GRID_W, NA_KR * GRID_W), lambda d: (d, 0, 0, 0)),
        out_shape=jax.ShapeDtypeStruct((NA_KR, HP, 2 * GRID_W, NA_KR * GRID_W), f32),
        compiler_params=pltpu.CompilerParams(vmem_limit_bytes=VMEM_LIMIT),
        name="nat_bias",
    )(rp)


ML_SCALE = (ML_NOPE + ML_ROPE) ** -0.5 * math.log2(math.e)
ML_QW = ML_H * LANE
ML_P1 = 2 * ML_RANK + 2 * LANE
ML_CK = 256


def _mla_proj_kernel(x_ref, mod_ref, g_ref, w1_ref, qg_ref, kvg_ref, wqa_ref, wqb_ref, wk_ref, wv_ref,
                     cos_ref, sin_ref, q_out, k_out, v_out):
    mod = mod_ref[...]
    h = _norm_mod(x_ref[...], g_ref[...], mod[:, 0:D], mod[:, D:2 * D]).astype(bf16)
    p = _dot(h, w1_ref[...])
    cq = _rmsnorm(p[:, 0:ML_RANK], qg_ref[...]).astype(bf16)
    ckv = _rmsnorm(p[:, ML_RANK:2 * ML_RANK], kvg_ref[...]).astype(bf16)
    cos = cos_ref[...]
    sin = sin_ref[...]
    krope = p[:, 2 * ML_RANK:2 * ML_RANK + LANE] * cos + p[:, 2 * ML_RANK + LANE:] * sin
    for c in range(ML_QW // ML_CK):
        lo = c * ML_CK
        qa = _dot(cq, wqa_ref[:, lo:lo + ML_CK])
        qb = _dot(cq, wqb_ref[:, lo:lo + ML_CK])
        kn = _dot(ckv, wk_ref[:, lo:lo + ML_CK])
        for e in range(ML_CK // LANE):
            ls = slice(e * LANE, (e + 1) * LANE)
            q_out[:, lo + e * LANE:lo + (e + 1) * LANE] = (
                (qa[:, ls] * cos + qb[:, ls] * sin) * ML_SCALE).astype(bf16)
            k_out[:, lo + e * LANE:lo + (e + 1) * LANE] = (kn[:, ls] + krope).astype(bf16)
    ones_lane = lax.broadcasted_iota(jnp.int32, (TM, ML_CK), 1) % LANE >= ML_V
    for c in range(ML_QW // ML_CK):
        lo = c * ML_CK
        vv = _dot(ckv, wv_ref[:, lo:lo + ML_CK])
        v_out[:, lo:lo + ML_CK] = jnp.where(ones_lane, 1.0, vv).astype(bf16)


def _mla_proj_call(x, modv, layer, g, w1, qg, kvg, wqa, wqb, wk, wv, cos_t, sin_t):
    rope_idx = lambda t: (jnp.where(t < N_LAT_TILES, t % TILES_PER_SEQ, TILES_PER_SEQ), 0)
    row = lambda w: pl.BlockSpec((TM, w), lambda t: (t, 0))
    return pl.pallas_call(
        _mla_proj_kernel,
        grid=(N_TILES,),
        in_specs=[row(D), _mod_spec(layer), _resident((1, D)), _resident((D, ML_P1)),
                  _resident((1, ML_RANK)), _resident((1, ML_RANK)),
                  _resident((ML_RANK, ML_QW)), _resident((ML_RANK, ML_QW)),
                  _resident((ML_RANK, ML_QW)), _resident((ML_RANK, ML_QW)),
                  pl.BlockSpec((TM, LANE), rope_idx), pl.BlockSpec((TM, LANE), rope_idx)],
        out_specs=[row(ML_QW), row(ML_QW), row(ML_QW)],
        out_shape=[jax.ShapeDtypeStruct((NT, ML_QW), bf16)] * 3,
        compiler_params=pltpu.CompilerParams(vmem_limit_bytes=VMEM_LIMIT),
        name="mla_proj",
    )(x, modv, g.reshape(1, D), w1, qg.reshape(1, ML_RANK), kvg.reshape(1, ML_RANK),
      wqa, wqb, wk, wv, cos_t, sin_t)


ML_TQ = 1024
ML_TK = 2048


def _mla_attn_kernel(q_ref, kl_ref, kc_ref, vl_ref, vc_ref, out_ref):
    qs = [q_ref[:, e * LANE:(e + 1) * LANE] for e in range(2)]
    n_lat = S // ML_TK

    def chunk(lat_ref, ctx_ref, c):
        lat = lat_ref[c * ML_TK:(c + 1) * ML_TK, :]
        return lat if c < n_lat - 1 else jnp.concatenate([lat, ctx_ref[...]], axis=0)

    def keys(c):
        return chunk(kl_ref, kc_ref, c)

    def values(c):
        return chunk(vl_ref, vc_ref, c)

    def update(e, c, carry):
        m, acc = carry
        ls = slice(e * LANE, (e + 1) * LANE)
        s = _dot_nt(qs[e], keys(c)[:, ls])
        m_new = jnp.maximum(m, jnp.max(s, axis=-1, keepdims=True))
        alpha = jnp.exp2(m - m_new)
        p = jnp.exp2(s - m_new)
        acc = alpha * acc + _dot(p.astype(bf16), values(c)[:, ls])
        return m_new, acc

    init = (jnp.full((ML_TQ, 1), -jnp.inf, f32), jnp.zeros((ML_TQ, LANE), f32))
    carry = [init, init]
    for c in range(n_lat):
        carry = [update(e, c, carry[e]) for e in range(2)]
    a0, a1 = carry[0][1], carry[1][1]
    lane = lax.broadcasted_iota(jnp.int32, (ML_TQ, LANE), 1)
    num = jnp.where(lane < ML_V, a0, pltpu.roll(a1, ML_V, axis=1))
    den = jnp.where(lane < ML_V, pltpu.roll(a0, ML_V, axis=1), a1)
    out_ref[...] = (num / den).astype(bf16)


def _mla_attn_call(q, k, v):
    nq = S // ML_TQ
    ctx_blk = T_LAT // CTX
    return pl.pallas_call(
        _mla_attn_kernel,
        grid=(B, HP, nq),
        in_specs=[
            pl.BlockSpec((ML_TQ, 2 * LANE), lambda b, hp, i: (b * nq + i, hp)),
            pl.BlockSpec((S, 2 * LANE), lambda b, hp, i: (b, hp)),
            pl.BlockSpec((CTX, 2 * LANE), lambda b, hp, i: (ctx_blk + b, hp)),
            pl.BlockSpec((S, 2 * LANE), lambda b, hp, i: (b, hp)),
            pl.BlockSpec((CTX, 2 * LANE), lambda b, hp, i: (ctx_blk + b, hp)),
        ],
        out_specs=pl.BlockSpec((ML_TQ, LANE), lambda b, hp, i: (b * nq + i, hp)),
        out_shape=jax.ShapeDtypeStruct((T_LAT, D), bf16),
        compiler_params=pltpu.CompilerParams(vmem_limit_bytes=VMEM_LIMIT),
        name="mla_attn",
    )(q, k, k, v, v)


def _mla_weights(w_dq, w_uq, w_dkv, w_ukv):
    z = lambda *s: jnp.zeros(s, f32)
    uq = w_uq.reshape(ML_RANK, ML_H, ML_NOPE + ML_ROPE)
    nope, r1, r2 = uq[..., :ML_NOPE], uq[..., ML_NOPE:ML_NOPE + 16], uq[..., ML_NOPE + 16:]
    pad = z(ML_RANK, ML_H, LANE - ML_NOPE - ML_ROPE)
    wqa = jnp.concatenate([nope, r1, r2, pad], axis=-1).reshape(ML_RANK, ML_QW)
    wqb = jnp.concatenate([jnp.zeros_like(nope), -r2, r1, pad], axis=-1).reshape(ML_RANK, ML_QW)
    ukv = w_ukv.reshape(ML_RANK, ML_H, ML_NOPE + ML_V)
    wk = jnp.concatenate([ukv[..., :ML_NOPE], z(ML_RANK, ML_H, LANE - ML_NOPE)], axis=-1).reshape(ML_RANK, ML_QW)
    wv = jnp.concatenate([ukv[..., ML_NOPE:], z(ML_RANK, ML_H, LANE - ML_V)], axis=-1).reshape(ML_RANK, ML_QW)
    wr = w_dkv[:, ML_RANK:]
    rope1 = jnp.concatenate([z(D, ML_NOPE), wr, z(D, LANE - ML_NOPE - ML_ROPE)], axis=-1)
    rope2 = jnp.concatenate([z(D, ML_NOPE), -wr[:, 16:], wr[:, :16], z(D, LANE - ML_NOPE - ML_ROPE)], axis=-1)
    w1 = jnp.concatenate([w_dq, w_dkv[:, :ML_RANK], rope1, rope2], axis=-1)
    return tuple(w.astype(bf16) for w in (w1, wqa, wqb, wk, wv))


def _rope_tables():
    n_freq = ML_ROPE // 4
    freq = 10000.0 ** (-jnp.arange(n_freq, dtype=f32) / n_freq)
    t = jnp.arange(S)
    row = (t // GRID_W).astype(f32)
    col = (t % GRID_W).astype(f32)
    ang = jnp.concatenate([row[:, None] * freq, col[:, None] * freq], axis=-1)
    cos, sin = jnp.cos(ang), jnp.sin(ang)
    pad = jnp.zeros((S, LANE - ML_NOPE - ML_ROPE), f32)
    cos_l = jnp.concatenate([jnp.ones((S, ML_NOPE), f32), cos, cos, pad], axis=-1)
    sin_l = jnp.concatenate([jnp.zeros((S, ML_NOPE), f32), sin, sin, pad], axis=-1)
    cos_c = jnp.concatenate([jnp.ones((TM, ML_NOPE + ML_ROPE), f32), pad[:TM]], axis=-1)
    return jnp.concatenate([cos_l, cos_c], axis=0), jnp.concatenate([sin_l, jnp.zeros((TM, LANE), f32)], axis=0)


FN_G = 4
FN_C = D // FN_G
FN_R = 64
FN_J = 16
assert FN_R * FN_R == S


def _regroup(x, outer, inner):
    return x.reshape(outer, inner, x.shape[-1]).swapaxes(0, 1).reshape(x.shape)


def _fnet_s1_kernel(x_ref, mod_ref, g_ref, l1_ref, tr_out, ti_out):
    mod = mod_ref[...]
    x = x_ref[...].reshape(FN_R * FN_J, D)
    h = _norm_mod(x, g_ref[...], mod[:, 0:D], mod[:, D:2 * D])
    h = _regroup(h, FN_R, FN_J).astype(bf16)
    t = jnp.concatenate([_dot(l1_ref[...], h[j * FN_R:(j + 1) * FN_R]) for j in range(FN_J)], axis=0)
    t = _regroup(t, FN_J, 2 * FN_R)
    half = FN_R * FN_J
    tr_out[...] = t[:half].astype(bf16).reshape(FN_R, FN_J, D)
    ti_out[...] = t[half:].astype(bf16).reshape(FN_R, FN_J, D)


def _fnet_s1_call(x4, modv, layer, g, l1):
    blk = pl.BlockSpec((None, FN_R, FN_J, D), lambda b, jb: (b, 0, jb, 0))
    t_shape = jax.ShapeDtypeStruct((B, FN_R, FN_R, D), bf16)
    return pl.pallas_call(
        _fnet_s1_kernel,
        grid=(B, FN_R // FN_J),
        in_specs=[blk, pl.BlockSpec((None, 1, N_MOD * D), lambda b, jb: (layer * 8 + b, 0, 0)),
                  _resident((1, D)), _resident((2 * FN_R, FN_R))],
        out_specs=[blk, blk],
        out_shape=[t_shape, t_shape],
        compiler_params=pltpu.CompilerParams(vmem_limit_bytes=VMEM_LIMIT),
        name="fnet_stage1",
    )(x4, modv, g.reshape(1, D), l1)


def _fnet_s2_kernel(tr_ref, ti_ref, wc_ref, g_ref, out_ref):
    kb = pl.program_id(1)
    tr = tr_ref[...].reshape(FN_J * FN_R, D)
    ti = ti_ref[...].reshape(FN_J * FN_R, D)
    ur, ui = [], []
    for gi in range(FN_G):
        ls = slice(gi * FN_C, (gi + 1) * FN_C)
        u = _dot(jnp.concatenate([tr[:, ls], ti[:, ls]], axis=1), wc_ref[...])
        ur.append(u[:, :FN_C].astype(bf16))
        ui.append(u[:, FN_C:].astype(bf16))
    ur = jnp.concatenate(ur, axis=1)
    ui = jnp.concatenate(ui, axis=1)
    ys = []
    for j in range(FN_J):
        rows = slice(j * FN_R, (j + 1) * FN_R)
        t = jnp.concatenate([ur[rows], ui[rows]], axis=0)
        ys.append(_dot(g_ref[kb * FN_J + j], t))
    y = _regroup(jnp.concatenate(ys, axis=0), FN_J, FN_R)
    out_ref[...] = (y * (1.0 / math.sqrt(S * FN_C))).astype(bf16).reshape(FN_R, FN_J, D)


def _fnet_s2_call(tr, ti, wc, gtab):
    t_blk = pl.BlockSpec((None, FN_J, FN_R, D), lambda b, kb: (b, kb, 0, 0))
    return pl.pallas_call(
        _fnet_s2_kernel,
        grid=(B, FN_R // FN_J),
        in_specs=[t_blk, t_blk, _resident((2 * FN_C, 2 * FN_C)), _resident((FN_R, FN_R, 2 * FN_R))],
        out_specs=pl.BlockSpec((None, FN_R, FN_J, D), lambda b, kb: (b, 0, kb, 0)),
        out_shape=jax.ShapeDtypeStruct((B, FN_R, FN_R, D), bf16),
        compiler_params=pltpu.CompilerParams(vmem_limit_bytes=VMEM_LIMIT),
        name="fnet_stage2",
    )(tr, ti, wc, gtab)


def _cos_sin(num, den):
    ang = (np.asarray(num) % den).astype(np.float64) * (2.0 * np.pi / den)
    return np.cos(ang), np.sin(ang)


def _fnet_tables():
    ch = np.arange(FN_C)
    cc, sc = _cos_sin(ch[:, None] * ch[None, :], FN_C)
    r = np.arange(FN_R)
    c1, s1 = _cos_sin(r[:, None] * r[None, :], FN_R)
    l1 = np.concatenate([c1, -s1], axis=0)
    wc = np.block([[cc, -sc], [sc, cc]])
    k = r[:, None, None] + FN_R * r[None, :, None]
    gc, gs = _cos_sin(k * r[None, None, :], S)
    gtab = np.concatenate([gc, gs], axis=-1)
    return tuple(jnp.asarray(t, dtype=f32).astype(bf16) for t in (l1, wc, gtab))


def kernel(x, c, ctx, c_ctx, mod_w, mod_b, mix_norm_g, ffn_norm_g, conv_w_in, conv_w, conv_w_out,
           nat_w_qkv, nat_rpb, nat_w_o, mla_w_dq, mla_q_norm_g, mla_w_uq, mla_w_dkv, mla_kv_norm_g,
           mla_w_ukv, mla_w_o, fnet_w_o, ffn_w_in, ffn_w_out, final_norm_g):
    cc8 = jnp.concatenate([c, c_ctx[None, :], jnp.zeros((8 - B - 1, D), f32)], axis=0)
    modv = _mod_call(cc8, mod_w, mod_b).reshape(DEPTH * 8, 1, N_MOD * D)

    w_in = ffn_w_in.astype(bf16)
    w_out = ffn_w_out.astype(bf16)
    x_lat = x.reshape(T_LAT, D)
    x_ctx = ctx.reshape(T_CTX, D)

    cw8 = jnp.concatenate([conv_w[0], jnp.zeros((5, D), f32)], axis=0)
    bz = _conv_call(x_lat, x_ctx, modv, 0, mix_norm_g[0], conv_w_in[0].astype(bf16), cw8)
    xs = _tail_call((x_lat, x_ctx), bz, modv, 0, conv_w_out[0].astype(bf16), ffn_norm_g[0], w_in, w_out,
                    N_TILES)

    qkv = _qkv_call(xs, modv, 1, mix_norm_g[1], nat_w_qkv[0].astype(bf16))
    o = _nat_call(qkv, _nat_bias_table(nat_rpb[0]))
    xs = _tail_call(xs, o, modv, 1, nat_w_o[0].astype(bf16), ffn_norm_g[1], w_in, w_out, N_TILES)

    w1, wqa, wqb, wk, wv = _mla_weights(mla_w_dq[0], mla_w_uq[0], mla_w_dkv[0], mla_w_ukv[0])
    cos_t, sin_t = _rope_tables()
    q, k, v = _mla_proj_call(xs, modv, 2, mix_norm_g[2], w1, mla_q_norm_g[0], mla_kv_norm_g[0],
                             wqa, wqb, wk, wv, cos_t, sin_t)
    o = _mla_attn_call(q, k, v)
    xl = _tail_call(xs, o, modv, 2, mla_w_o[0].astype(bf16), ffn_norm_g[2], w_in, w_out, N_LAT_TILES)

    l1_t, wc_t, g_t = _fnet_tables()
    tr, ti = _fnet_s1_call(xl.reshape(B, FN_R, FN_R, D), modv, 3, mix_norm_g[3], l1_t)
    o = _fnet_s2_call(tr, ti, wc_t, g_t).reshape(T_LAT, D)
    out = _tail_call(xl, o, modv, 3, fnet_w_o[0].astype(bf16), ffn_norm_g[3], w_in, w_out,
                     N_LAT_TILES, final_g=final_norm_g)
    return out.reshape(B, S, D)
```

```python
import functools
import math

import numpy as np
import jax
import jax.numpy as jnp
from jax import lax
from jax.experimental import pallas as pl
from jax.experimental.pallas import tpu as pltpu

D = 1024
B = 4
S = 4096
CTX = 256
DEPTH = 4
GRID_W = 64
N_MOD = 6
F = 2816
EPS = 1e-6

T_LAT = B * S
T_CTX = B * CTX
NT = T_LAT + T_CTX

NA_H = 16
NA_KR = 8
NA_KC = 16
ML_H = 16
ML_RANK = 256
ML_NOPE = 64
ML_ROPE = 32
ML_V = 64
HP = 8
LANE = 128
MASK_NEG = -1e30

TM = 512
N_TILES = NT // TM
N_LAT_TILES = T_LAT // TM
TILES_PER_SEQ = S // TM
FFN_CK = 256
VMEM_LIMIT = 56 * 1024 * 1024

f32 = jnp.float32
bf16 = jnp.bfloat16


def _resident(shape):
    nd = len(shape)
    return pl.BlockSpec(shape, lambda *_: (0,) * nd, pipeline_mode=pl.Buffered(1))


def _mod_spec(layer):
    return pl.BlockSpec((None, 1, N_MOD * D), lambda t: (layer * 8 + t // TILES_PER_SEQ, 0, 0))


def _rmsnorm(x, g):
    ms = jnp.mean(x * x, axis=-1, keepdims=True)
    return (x * lax.rsqrt(ms + EPS)) * g


def _norm_mod(x, g, shift, scale):
    return _rmsnorm(x, g) * (1.0 + scale) + shift


def _dot(a, b):
    return jnp.dot(a, b, preferred_element_type=f32)


def _dot_nt(a, b):
    return lax.dot_general(a, b, (((1,), (1,)), ((), ())), preferred_element_type=f32)


def _mod_kernel(cc_ref, w_ref, b_ref, out_ref):
    cc = cc_ref[...]
    s = (cc * jax.nn.sigmoid(cc)).astype(bf16)
    out_ref[...] = _dot(s, w_ref[...].astype(bf16)) + b_ref[...]


def _mod_call(cc8, mod_w, mod_b):
    tn = 1536
    return pl.pallas_call(
        _mod_kernel,
        grid=(DEPTH, N_MOD * D // tn),
        in_specs=[
            pl.BlockSpec((8, D), lambda l, j: (0, 0)),
            pl.BlockSpec((None, D, tn), lambda l, j: (l, 0, j)),
            pl.BlockSpec((None, 1, tn), lambda l, j: (l, 0, j)),
        ],
        out_specs=pl.BlockSpec((None, 8, tn), lambda l, j: (l, 0, j)),
        out_shape=jax.ShapeDtypeStruct((DEPTH, 8, N_MOD * D), f32),
        compiler_params=pltpu.CompilerParams(vmem_limit_bytes=VMEM_LIMIT),
        name="mod_vectors",
    )(cc8, mod_w, mod_b.reshape(DEPTH, 1, N_MOD * D))


TAIL_SUB = 2
TMT = TAIL_SUB * TM


def _mod_spec_steps(layer):
    return pl.BlockSpec((None, 1, N_MOD * D), lambda t: (layer * 8 + t // (S // TMT), 0, 0))


def _tile_rows(lat_ref, ctx_ref, n_lat_steps):
    return jnp.where(pl.program_id(0) < n_lat_steps, lat_ref[...], ctx_ref[...])


def _split_specs(block_rows, lat_idx, ctx_idx, n_lat_steps):
    tl = lambda t: jnp.minimum(t, n_lat_steps - 1)
    tc = lambda t: jnp.maximum(t - n_lat_steps, 0)
    return [pl.BlockSpec((block_rows, D), lambda t: (lat_idx(tl(t)), 0)),
            pl.BlockSpec((block_rows, D), lambda t: (ctx_idx(tc(t)), 0))]


def _tail_kernel(*refs, final, split):
    if split:
        xl_ref, xc_ref, *refs = refs
        x = _tile_rows(xl_ref, xc_ref, T_LAT // TMT)
    else:
        x_ref, *refs = refs
        x = x_ref[...]
    o_ref, mod_ref, wo_ref, g_ref, win_ref, wout_ref, *rest = refs
    if final:
        fg_ref, out_ref, acc_ref = rest
    else:
        out_ref, acc_ref = rest
    mod = mod_ref[...]
    g1 = mod[:, 2 * D:3 * D]
    sh2 = mod[:, 3 * D:4 * D]
    sc2 = mod[:, 4 * D:5 * D]
    g2 = mod[:, 5 * D:6 * D]
    tiles = [slice(u * TM, (u + 1) * TM) for u in range(TAIL_SUB)]
    x1 = [x[rows] + g1 * _dot(o_ref[rows, :], wo_ref[...]) for rows in tiles]
    h = [_norm_mod(v, g_ref[...], sh2, sc2).astype(bf16) for v in x1]
    for u in range(TAIL_SUB):
        for c in range(F // FFN_CK):
            lo = c * FFN_CK
            gate = _dot(h[u], win_ref[:, lo:lo + FFN_CK])
            up = _dot(h[u], win_ref[:, F + lo:F + lo + FFN_CK])
            a = (gate * jax.nn.sigmoid(gate) * up).astype(bf16)
            y = _dot(a, wout_ref[lo:lo + FFN_CK, :])
            if c == 0:
                acc_ref[u] = y
            else:
                acc_ref[u] += y
    for u, rows in enumerate(tiles):
        x2 = x1[u] + g2 * acc_ref[u]
        if final:
            x2 = _rmsnorm(x2, fg_ref[...])
        out_ref[rows, :] = x2


def _tail_call(x, o, modv, layer, wo, g, win, wout, n_rows, final_g=None):
    final = final_g is not None
    split = isinstance(x, tuple)
    n_steps = n_rows // TMT
    row = lambda w: pl.BlockSpec((TMT, w), lambda t: (t, 0))
    slab = lambda r, c: pl.BlockSpec((None, r, c), lambda t: (layer, 0, 0), pipeline_mode=pl.Buffered(1))
    x_specs = _split_specs(TMT, lambda i: i, lambda i: i, T_LAT // TMT) if split else [row(D)]
    in_specs = x_specs + [row(D), _mod_spec_steps(layer), _resident((D, D)), _resident((1, D)),
                          slab(D, 2 * F), slab(F, D)]
    args = list(x if split else (x,)) + [o, modv, wo, g.reshape(1, D), win, wout]
    if final:
        in_specs.append(_resident((1, D)))
        args.append(final_g.reshape(1, D))
    return pl.pallas_call(
        functools.partial(_tail_kernel, final=final, split=split),
        grid=(n_steps,),
        in_specs=in_specs,
        out_specs=row(D),
        out_shape=jax.ShapeDtypeStruct((n_rows, D), f32),
        scratch_shapes=[pltpu.VMEM((TAIL_SUB, TM, D), f32)],
        compiler_params=pltpu.CompilerParams(vmem_limit_bytes=VMEM_LIMIT),
        name=f"tail_l{layer}",
    )(*args)


CONV_HALO = 16
CONV_CK = 256


def _conv_kernel(lp_ref, cp_ref, lx_ref, cx_ref, ln_ref, cn_ref, mod_ref, g_ref, win_ref, cw_ref, out_ref):
    t = pl.program_id(0)
    mod = mod_ref[...]
    sh1 = mod[:, 0:D]
    sc1 = mod[:, D:2 * D]
    g = g_ref[...]
    pieces = [_tile_rows(a, b, N_LAT_TILES) for a, b in ((lp_ref, cp_ref), (lx_ref, cx_ref), (ln_ref, cn_ref))]
    h = jnp.concatenate([_norm_mod(p, g, sh1, sc1).astype(bf16) for p in pieces], axis=0)
    rows = TM + 2 * CONV_HALO
    seq_len = jnp.where(t < N_LAT_TILES, S, CTX)
    grow = t * TM - CONV_HALO + lax.broadcasted_iota(jnp.int32, (rows, CONV_CK), 0)
    pos = jnp.bitwise_and(grow, seq_len - 1)
    has_prev = pos != 0
    has_next = pos != seq_len - 1
    for j in range(D // CONV_CK):
        lo = j * CONV_CK
        bg = _dot(h, win_ref[:, lo:lo + CONV_CK])
        cg = _dot(h, win_ref[:, D + lo:D + lo + CONV_CK])
        v = _dot(h, win_ref[:, 2 * D + lo:2 * D + lo + CONV_CK])
        u = cg * v
        u_prev = jnp.where(has_prev, pltpu.roll(u, 1, axis=0), 0.0)
        u_next = jnp.where(has_next, pltpu.roll(u, rows - 1, axis=0), 0.0)
        z = (cw_ref[0:1, lo:lo + CONV_CK] * u_prev + cw_ref[1:2, lo:lo + CONV_CK] * u
             + cw_ref[2:3, lo:lo + CONV_CK] * u_next)
        bz = bg * z
        out_ref[:, lo:lo + CONV_CK] = bz[CONV_HALO:CONV_HALO + TM].astype(bf16)


def _conv_call(x_lat, x_ctx, modv, layer, g, win, cw):
    hb = TM // CONV_HALO
    prev_idx = lambda i: jnp.maximum(i * hb - 1, 0)
    next_idx = lambda rows: (lambda i: jnp.minimum((i + 1) * hb, rows // CONV_HALO - 1))
    prev_l, prev_c = _split_specs(CONV_HALO, prev_idx, prev_idx, N_LAT_TILES)
    next_l, next_c = _split_specs(CONV_HALO, next_idx(T_LAT), next_idx(T_CTX), N_LAT_TILES)
    main_l, main_c = _split_specs(TM, lambda i: i, lambda i: i, N_LAT_TILES)
    return pl.pallas_call(
        _conv_kernel,
        grid=(N_TILES,),
        in_specs=[prev_l, prev_c, main_l, main_c, next_l, next_c,
                  _mod_spec(layer), _resident((1, D)), _resident((D, 3 * D)), _resident((8, D))],
        out_specs=pl.BlockSpec((TM, D), lambda t: (t, 0)),
        out_shape=jax.ShapeDtypeStruct((NT, D), bf16),
        compiler_params=pltpu.CompilerParams(vmem_limit_bytes=VMEM_LIMIT),
        name="conv_mixer",
    )(x_lat, x_ctx, x_lat, x_ctx, x_lat, x_ctx, modv, g.reshape(1, D), win, cw)


QKV_CK = 512
NA_SCALE = 0.125


def _qkv_kernel(x_ref, mod_ref, g_ref, w_ref, out_ref):
    mod = mod_ref[...]
    tiles = [slice(u * TM, (u + 1) * TM) for u in range(TAIL_SUB)]
    hs = [_norm_mod(x_ref[rows, :], g_ref[...], mod[:, 0:D], mod[:, D:2 * D]).astype(bf16) for rows in tiles]
    for h, rows in zip(hs, tiles):
        for c in range(3 * D // QKV_CK):
            lo = c * QKV_CK
            r = _dot(h, w_ref[:, lo:lo + QKV_CK])
            if lo < D:
                r = r * NA_SCALE
            out_ref[rows, lo:lo + QKV_CK] = r.astype(bf16)


def _qkv_call(x, modv, layer, g, w):
    return pl.pallas_call(
        _qkv_kernel,
        grid=(NT // TMT,),
        in_specs=[pl.BlockSpec((TMT, D), lambda t: (t, 0)), _mod_spec_steps(layer),
                  _resident((1, D)), _resident((D, 3 * D))],
        out_specs=pl.BlockSpec((TMT, 3 * D), lambda t: (t, 0)),
        out_shape=jax.ShapeDtypeStruct((NT, 3 * D), bf16),
        compiler_params=pltpu.CompilerParams(vmem_limit_bytes=VMEM_LIMIT),
        name="nat_qkv",
    )(x, modv, g.reshape(1, D), w)


def _stack_pair(q2):
    lane = lax.broadcasted_iota(jnp.int32, q2.shape, 1)
    zero = jnp.zeros_like(q2)
    return jnp.concatenate([jnp.where(lane < 64, q2, zero), jnp.where(lane >= 64, q2, zero)], axis=0)


def _unstack_pair(o, n):
    lane = lax.broadcasted_iota(jnp.int32, (n, LANE), 1)
    return jnp.where(lane < 64, o[:n], o[n:])


NA_ROWS = S // GRID_W
NA_SUB = 4
NA_WIN_STEPS = NA_ROWS // NA_SUB
NA_CTX_STEPS = CTX // (NA_SUB * GRID_W)


def _nat_kernel(q_ref, kb_ref, vb_ref, kc_ref, vc_ref, bias_ref, out_ref):
    step = pl.program_id(1)
    lanes = [slice(hp * LANE, (hp + 1) * LANE) for hp in range(HP)]
    nband = NA_KR * GRID_W

    def attend(with_window):
        scores = []
        for sub in range(NA_SUB):
            rows = slice(sub * GRID_W, (sub + 1) * GRID_W)
            r = step * NA_SUB + sub
            r0 = jnp.clip(r - NA_KR // 2, 0, NA_ROWS - NA_KR)
            off = pl.multiple_of(r0 * GRID_W, GRID_W)
            for hp, ls in enumerate(lanes):
                qst = _stack_pair(q_ref[rows, ls])
                s_c = _dot_nt(qst, kc_ref[:, ls])
                s_w = None
                if with_window:
                    s_w = _dot_nt(qst, kb_ref[pl.ds(off, nband), ls]) + bias_ref[r0 - r + NA_KR - 1, hp]
                scores.append((s_w, s_c, off, rows, ls))
        probs = []
        for s_w, s_c, off, rows, ls in scores:
            m = jnp.max(s_c, axis=-1, keepdims=True)
            if with_window:
                m = jnp.maximum(m, jnp.max(s_w, axis=-1, keepdims=True))
            p_c = jnp.exp(s_c - m)
            l = jnp.sum(p_c, axis=-1, keepdims=True)
            p_w = None
            if with_window:
                p_w = jnp.exp(s_w - m)
                l = l + jnp.sum(p_w, axis=-1, keepdims=True)
                p_w = p_w.astype(bf16)
            probs.append((p_w, p_c.astype(bf16), l, off, rows, ls))
        for p_w, p_c, l, off, rows, ls in probs:
            o = _dot(p_c, vc_ref[:, ls])
            if with_window:
                o = o + _dot(p_w, vb_ref[pl.ds(off, nband), ls])
            out_ref[rows, ls] = _unstack_pair(o / l, GRID_W).astype(bf16)

    @pl.when(step < NA_WIN_STEPS)
    def _():
        attend(True)

    @pl.when(step >= NA_WIN_STEPS)
    def _():
        attend(False)


def _nat_call(qkv, bias):
    tq = NA_SUB * GRID_W

    def q_idx(b, s):
        ctx_blk = T_LAT // tq + b * NA_CTX_STEPS + (s - NA_WIN_STEPS)
        return (jnp.where(s < NA_WIN_STEPS, b * NA_WIN_STEPS + s, ctx_blk), 0)

    once = dict(pipeline_mode=pl.Buffered(1))
    ctx_blk = T_LAT // CTX
    return pl.pallas_call(
        _nat_kernel,
        grid=(B, NA_WIN_STEPS + NA_CTX_STEPS),
        in_specs=[
            pl.BlockSpec((tq, D), q_idx),
            pl.BlockSpec((S, D), lambda b, s: (b, 1), **once),
            pl.BlockSpec((S, D), lambda b, s: (b, 2), **once),
            pl.BlockSpec((CTX, D), lambda b, s: (ctx_blk + b, 1)),
            pl.BlockSpec((CTX, D), lambda b, s: (ctx_blk + b, 2)),
            _resident((NA_KR, HP, 2 * GRID_W, NA_KR * GRID_W)),
        ],
        out_specs=pl.BlockSpec((tq, D), q_idx),
        out_shape=jax.ShapeDtypeStruct((NT, D), bf16),
        compiler_params=pltpu.CompilerParams(vmem_limit_bytes=VMEM_LIMIT),
        name="nat_attn",
    )(qkv, qkv, qkv, qkv, qkv, bias)


def _nat_bias_kernel(rpb_ref, out_ref):
    d0 = pl.program_id(0)
    q = lax.broadcasted_iota(jnp.int32, (GRID_W, LANE), 0)
    lane = lax.broadcasted_iota(jnp.int32, (GRID_W, LANE), 1)
    j = lane % GRID_W
    c0 = jnp.clip(q - NA_KC // 2, 0, GRID_W - NA_KC)
    valid = (j >= c0) & (j < c0 + NA_KC)
    for h in range(NA_H):
        hp, e = divmod(h, 2)
        for m in range(NA_KR // 2):
            halves = []
            for u in range(2):
                row = jnp.broadcast_to(rpb_ref[h, pl.ds(d0 + 2 * m + u, 1), :], (GRID_W, LANE))
                shift = (LANE - (NA_KC - 1) + u * GRID_W) % LANE
                halves.append(pltpu.roll(row, shift, axis=1, stride=1, stride_axis=0))
            t = jnp.where(lane < GRID_W, halves[0], halves[1])
            out_ref[hp, e * GRID_W:(e + 1) * GRID_W, m * LANE:(m + 1) * LANE] = jnp.where(valid, t, MASK_NEG)


def _nat_bias_table(rpb):
    rows, taps = 2 * NA_KR - 1, 2 * NA_KC - 1
    rp = jnp.pad(rpb, ((0, 0), (0, 2 * NA_KR - rows), (0, LANE - taps)))
    return pl.pallas_call(
        _nat_bias_kernel,
        grid=(NA_KR,),
        in_specs=[pl.BlockSpec((NA_H, 2 * NA_KR, LANE), lambda d: (0, 0, 0))],
        out_specs=pl.BlockSpec((None, HP, 2 * GRID_W, NA_KR * GRID_W), lambda d: (d, 0, 0, 0)),
        out_shape=jax.ShapeDtypeStruct((NA_KR, HP, 2 * GRID_W, NA_KR * GRID_W), f32),
        compiler_params=pltpu.CompilerParams(vmem_limit_bytes=VMEM_LIMIT),
        name="nat_bias",
    )(rp)


ML_SCALE = (ML_NOPE + ML_ROPE) ** -0.5 * math.log2(math.e)
ML_QW = ML_H * LANE
ML_P1 = 2 * ML_RANK + 2 * LANE
ML_CK = 256


def _mla_proj_kernel(x_ref, mod_ref, g_ref, w1_ref, qg_ref, kvg_ref, wqa_ref, wqb_ref, wk_ref, wv_ref,
                     cos_ref, sin_ref, q_out, k_out, v_out):
    mod = mod_ref[...]
    tiles = [slice(u * TM, (u + 1) * TM) for u in range(TAIL_SUB)]
    lat = []
    for rows in tiles:
        h = _norm_mod(x_ref[rows, :], g_ref[...], mod[:, 0:D], mod[:, D:2 * D]).astype(bf16)
        p = _dot(h, w1_ref[...])
        cq = _rmsnorm(p[:, 0:ML_RANK], qg_ref[...]).astype(bf16)
        ckv = _rmsnorm(p[:, ML_RANK:2 * ML_RANK], kvg_ref[...]).astype(bf16)
        cos = cos_ref[rows, :]
        sin = sin_ref[rows, :]
        krope = p[:, 2 * ML_RANK:2 * ML_RANK + LANE] * cos + p[:, 2 * ML_RANK + LANE:] * sin
        lat.append((cq, ckv, cos, sin, krope))
    ones_lane = lax.broadcasted_iota(jnp.int32, (TM, ML_CK), 1) % LANE >= ML_V
    for (cq, ckv, cos, sin, krope), rows in zip(lat, tiles):
        for c in range(ML_QW // ML_CK):
            lo = c * ML_CK
            qa = _dot(cq, wqa_ref[:, lo:lo + ML_CK])
            qb = _dot(cq, wqb_ref[:, lo:lo + ML_CK])
            kn = _dot(ckv, wk_ref[:, lo:lo + ML_CK])
            for e in range(ML_CK // LANE):
                ls = slice(e * LANE, (e + 1) * LANE)
                q_out[rows, lo + e * LANE:lo + (e + 1) * LANE] = (
                    (qa[:, ls] * cos + qb[:, ls] * sin) * ML_SCALE).astype(bf16)
                k_out[rows, lo + e * LANE:lo + (e + 1) * LANE] = (kn[:, ls] + krope).astype(bf16)
        for c in range(ML_QW // ML_CK):
            lo = c * ML_CK
            vv = _dot(ckv, wv_ref[:, lo:lo + ML_CK])
            v_out[rows, lo:lo + ML_CK] = jnp.where(ones_lane, 1.0, vv).astype(bf16)


def _mla_proj_call(x, modv, layer, g, w1, qg, kvg, wqa, wqb, wk, wv, cos_t, sin_t):
    lat_steps, steps_per_seq = T_LAT // TMT, S // TMT
    rope_idx = lambda t: (jnp.where(t < lat_steps, t % steps_per_seq, steps_per_seq), 0)
    row = lambda w: pl.BlockSpec((TMT, w), lambda t: (t, 0))
    return pl.pallas_call(
        _mla_proj_kernel,
        grid=(NT // TMT,),
        in_specs=[row(D), _mod_spec_steps(layer), _resident((1, D)), _resident((D, ML_P1)),
                  _resident((1, ML_RANK)), _resident((1, ML_RANK)),
                  _resident((ML_RANK, ML_QW)), _resident((ML_RANK, ML_QW)),
                  _resident((ML_RANK, ML_QW)), _resident((ML_RANK, ML_QW)),
                  pl.BlockSpec((TMT, LANE), rope_idx), pl.BlockSpec((TMT, LANE), rope_idx)],
        out_specs=[row(ML_QW), row(ML_QW), row(ML_QW)],
        out_shape=[jax.ShapeDtypeStruct((NT, ML_QW), bf16)] * 3,
        compiler_params=pltpu.CompilerParams(vmem_limit_bytes=VMEM_LIMIT),
        name="mla_proj",
    )(x, modv, g.reshape(1, D), w1, qg.reshape(1, ML_RANK), kvg.reshape(1, ML_RANK),
      wqa, wqb, wk, wv, cos_t, sin_t)


ML_TQ = 1024
ML_TK = 2048


def _mla_attn_kernel(q_ref, kl_ref, kc_ref, vl_ref, vc_ref, out_ref):
    qs = [q_ref[:, e * LANE:(e + 1) * LANE] for e in range(2)]
    n_lat = S // ML_TK

    def chunk(lat_ref, ctx_ref, c):
        lat = lat_ref[c * ML_TK:(c + 1) * ML_TK, :]
        return lat if c < n_lat - 1 else jnp.concatenate([lat, ctx_ref[...]], axis=0)

    def keys(c):
        return chunk(kl_ref, kc_ref, c)

    def values(c):
        return chunk(vl_ref, vc_ref, c)

    def update(e, c, carry):
        m, acc = carry
        ls = slice(e * LANE, (e + 1) * LANE)
        s = _dot_nt(qs[e], keys(c)[:, ls])
        m_new = jnp.maximum(m, jnp.max(s, axis=-1, keepdims=True))
        alpha = jnp.exp2(m - m_new)
        p = jnp.exp2(s - m_new)
        acc = alpha * acc + _dot(p.astype(bf16), values(c)[:, ls])
        return m_new, acc

    init = (jnp.full((ML_TQ, 1), -jnp.inf, f32), jnp.zeros((ML_TQ, LANE), f32))
    carry = [init, init]
    for c in range(n_lat):
        carry = [update(e, c, carry[e]) for e in range(2)]
    a0, a1 = carry[0][1], carry[1][1]
    lane = lax.broadcasted_iota(jnp.int32, (ML_TQ, LANE), 1)
    num = jnp.where(lane < ML_V, a0, pltpu.roll(a1, ML_V, axis=1))
    den = jnp.where(lane < ML_V, pltpu.roll(a0, ML_V, axis=1), a1)
    out_ref[...] = (num / den).astype(bf16)


def _mla_attn_call(q, k, v):
    nq = S // ML_TQ
    ctx_blk = T_LAT // CTX
    return pl.pallas_call(
        _mla_attn_kernel,
        grid=(B, HP, nq),
        in_specs=[
            pl.BlockSpec((ML_TQ, 2 * LANE), lambda b, hp, i: (b * nq + i, hp)),
            pl.BlockSpec((S, 2 * LANE), lambda b, hp, i: (b, hp)),
            pl.BlockSpec((CTX, 2 * LANE), lambda b, hp, i: (ctx_blk + b, hp)),
            pl.BlockSpec((S, 2 * LANE), lambda b, hp, i: (b, hp)),
            pl.BlockSpec((CTX, 2 * LANE), lambda b, hp, i: (ctx_blk + b, hp)),
        ],
        out_specs=pl.BlockSpec((ML_TQ, LANE), lambda b, hp, i: (b * nq + i, hp)),
        out_shape=jax.ShapeDtypeStruct((T_LAT, D), bf16),
        compiler_params=pltpu.CompilerParams(vmem_limit_bytes=VMEM_LIMIT),
        name="mla_attn",
    )(q, k, k, v, v)


def _mla_weights(w_dq, w_uq, w_dkv, w_ukv):
    z = lambda *s: jnp.zeros(s, f32)
    uq = w_uq.reshape(ML_RANK, ML_H, ML_NOPE + ML_ROPE)
    nope, r1, r2 = uq[..., :ML_NOPE], uq[..., ML_NOPE:ML_NOPE + 16], uq[..., ML_NOPE + 16:]
    pad = z(ML_RANK, ML_H, LANE - ML_NOPE - ML_ROPE)
    wqa = jnp.concatenate([nope, r1, r2, pad], axis=-1).reshape(ML_RANK, ML_QW)
    wqb = jnp.concatenate([jnp.zeros_like(nope), -r2, r1, pad], axis=-1).reshape(ML_RANK, ML_QW)
    ukv = w_ukv.reshape(ML_RANK, ML_H, ML_NOPE + ML_V)
    wk = jnp.concatenate([ukv[..., :ML_NOPE], z(ML_RANK, ML_H, LANE - ML_NOPE)], axis=-1).reshape(ML_RANK, ML_QW)
    wv = jnp.concatenate([ukv[..., ML_NOPE:], z(ML_RANK, ML_H, LANE - ML_V)], axis=-1).reshape(ML_RANK, ML_QW)
    wr = w_dkv[:, ML_RANK:]
    rope1 = jnp.concatenate([z(D, ML_NOPE), wr, z(D, LANE - ML_NOPE - ML_ROPE)], axis=-1)
    rope2 = jnp.concatenate([z(D, ML_NOPE), -wr[:, 16:], wr[:, :16], z(D, LANE - ML_NOPE - ML_ROPE)], axis=-1)
    w1 = jnp.concatenate([w_dq, w_dkv[:, :ML_RANK], rope1, rope2], axis=-1)
    return tuple(w.astype(bf16) for w in (w1, wqa, wqb, wk, wv))


def _rope_tables():
    n_freq = ML_ROPE // 4
    freq = 10000.0 ** (-jnp.arange(n_freq, dtype=f32) / n_freq)
    t = jnp.arange(S)
    row = (t // GRID_W).astype(f32)
    col = (t % GRID_W).astype(f32)
    ang = jnp.concatenate([row[:, None] * freq, col[:, None] * freq], axis=-1)
    cos, sin = jnp.cos(ang), jnp.sin(ang)
    pad = jnp.zeros((S, LANE - ML_NOPE - ML_ROPE), f32)
    cos_l = jnp.concatenate([jnp.ones((S, ML_NOPE), f32), cos, cos, pad], axis=-1)
    sin_l = jnp.concatenate([jnp.zeros((S, ML_NOPE), f32), sin, sin, pad], axis=-1)
    cos_c = jnp.concatenate([jnp.ones((TMT, ML_NOPE + ML_ROPE), f32), pad[:TMT]], axis=-1)
    return jnp.concatenate([cos_l, cos_c], axis=0), jnp.concatenate([sin_l, jnp.zeros((TMT, LANE), f32)], axis=0)


FN_G = 4
FN_C = D // FN_G
FN_R = 64
FN_J = 16
assert FN_R * FN_R == S


def _regroup(x, outer, inner):
    return x.reshape(outer, inner, x.shape[-1]).swapaxes(0, 1).reshape(x.shape)


def _fnet_s1_kernel(x_ref, mod_ref, g_ref, l1_ref, tr_out, ti_out):
    mod = mod_ref[...]
    x = x_ref[...].reshape(FN_R * FN_J, D)
    h = _norm_mod(x, g_ref[...], mod[:, 0:D], mod[:, D:2 * D])
    h = _regroup(h, FN_R, FN_J).astype(bf16)
    t = jnp.concatenate([_dot(l1_ref[...], h[j * FN_R:(j + 1) * FN_R]) for j in range(FN_J)], axis=0)
    t = _regroup(t, FN_J, 2 * FN_R)
    half = FN_R * FN_J
    tr_out[...] = t[:half].astype(bf16).reshape(FN_R, FN_J, D)
    ti_out[...] = t[half:].astype(bf16).reshape(FN_R, FN_J, D)


def _fnet_s1_call(x4, modv, layer, g, l1):
    blk = pl.BlockSpec((None, FN_R, FN_J, D), lambda b, jb: (b, 0, jb, 0))
    t_shape = jax.ShapeDtypeStruct((B, FN_R, FN_R, D), bf16)
    return pl.pallas_call(
        _fnet_s1_kernel,
        grid=(B, FN_R // FN_J),
        in_specs=[blk, pl.BlockSpec((None, 1, N_MOD * D), lambda b, jb: (layer * 8 + b, 0, 0)),
                  _resident((1, D)), _resident((2 * FN_R, FN_R))],
        out_specs=[blk, blk],
        out_shape=[t_shape, t_shape],
        compiler_params=pltpu.CompilerParams(vmem_limit_bytes=VMEM_LIMIT),
        name="fnet_stage1",
    )(x4, modv, g.reshape(1, D), l1)


def _fnet_s2_kernel(tr_ref, ti_ref, wc_ref, g_ref, out_ref):
    kb = pl.program_id(1)
    tr = tr_ref[...].reshape(FN_J * FN_R, D)
    ti = ti_ref[...].reshape(FN_J * FN_R, D)
    ur, ui = [], []
    for gi in range(FN_G):
        ls = slice(gi * FN_C, (gi + 1) * FN_C)
        u = _dot(jnp.concatenate([tr[:, ls], ti[:, ls]], axis=1), wc_ref[...])
        ur.append(u[:, :FN_C].astype(bf16))
        ui.append(u[:, FN_C:].astype(bf16))
    ur = jnp.concatenate(ur, axis=1)
    ui = jnp.concatenate(ui, axis=1)
    ys = []
    for j in range(FN_J):
        rows = slice(j * FN_R, (j + 1) * FN_R)
        t = jnp.concatenate([ur[rows], ui[rows]], axis=0)
        ys.append(_dot(g_ref[kb * FN_J + j], t))
    y = _regroup(jnp.concatenate(ys, axis=0), FN_J, FN_R)
    out_ref[...] = (y * (1.0 / math.sqrt(S * FN_C))).astype(bf16).reshape(FN_R, FN_J, D)


def _fnet_s2_call(tr, ti, wc, gtab):
    t_blk = pl.BlockSpec((None, FN_J, FN_R, D), lambda b, kb: (b, kb, 0, 0))
    return pl.pallas_call(
        _fnet_s2_kernel,
        grid=(B, FN_R // FN_J),
        in_specs=[t_blk, t_blk, _resident((2 * FN_C, 2 * FN_C)), _resident((FN_R, FN_R, 2 * FN_R))],
        out_specs=pl.BlockSpec((None, FN_R, FN_J, D), lambda b, kb: (b, 0, kb, 0)),
        out_shape=jax.ShapeDtypeStruct((B, FN_R, FN_R, D), bf16),
        compiler_params=pltpu.CompilerParams(vmem_limit_bytes=VMEM_LIMIT),
        name="fnet_stage2",
    )(tr, ti, wc, gtab)


def _cos_sin(num, den):
    ang = (np.asarray(num) % den).astype(np.float64) * (2.0 * np.pi / den)
    return np.cos(ang), np.sin(ang)


def _fnet_tables():
    ch = np.arange(FN_C)
    cc, sc = _cos_sin(ch[:, None] * ch[None, :], FN_C)
    r = np.arange(FN_R)
    c1, s1 = _cos_sin(r[:, None] * r[None, :], FN_R)
    l1 = np.concatenate([c1, -s1], axis=0)
    wc = np.block([[cc, -sc], [sc, cc]])
    k = r[:, None, None] + FN_R * r[None, :, None]
    gc, gs = _cos_sin(k * r[None, None, :], S)
    gtab = np.concatenate([gc, gs], axis=-1)
    return tuple(jnp.asarray(t, dtype=f32).astype(bf16) for t in (l1, wc, gtab))


def kernel(x, c, ctx, c_ctx, mod_w, mod_b, mix_norm_g, ffn_norm_g, conv_w_in, conv_w, conv_w_out,
           nat_w_qkv, nat_rpb, nat_w_o, mla_w_dq, mla_q_norm_g, mla_w_uq, mla_w_dkv, mla_kv_norm_g,
           mla_w_ukv, mla_w_o, fnet_w_o, ffn_w_in, ffn_w_out, final_norm_g):
    cc8 = jnp.concatenate([c, c_ctx[None, :], jnp.zeros((8 - B - 1, D), f32)], axis=0)
    modv = _mod_call(cc8, mod_w, mod_b).reshape(DEPTH * 8, 1, N_MOD * D)

    w_in = ffn_w_in.astype(bf16)
    w_out = ffn_w_out.astype(bf16)
    x_lat = x.reshape(T_LAT, D)
    x_ctx = ctx.reshape(T_CTX, D)

    cw8 = jnp.concatenate([conv_w[0], jnp.zeros((5, D), f32)], axis=0)
    bz = _conv_call(x_lat, x_ctx, modv, 0, mix_norm_g[0], conv_w_in[0].astype(bf16), cw8)
    xs = _tail_call((x_lat, x_ctx), bz, modv, 0, conv_w_out[0].astype(bf16), ffn_norm_g[0], w_in, w_out,
                    NT)

    qkv = _qkv_call(xs, modv, 1, mix_norm_g[1], nat_w_qkv[0].astype(bf16))
    o = _nat_call(qkv, _nat_bias_table(nat_rpb[0]))
    xs = _tail_call(xs, o, modv, 1, nat_w_o[0].astype(bf16), ffn_norm_g[1], w_in, w_out, NT)

    w1, wqa, wqb, wk, wv = _mla_weights(mla_w_dq[0], mla_w_uq[0], mla_w_dkv[0], mla_w_ukv[0])
    cos_t, sin_t = _rope_tables()
    q, k, v = _mla_proj_call(xs, modv, 2, mix_norm_g[2], w1, mla_q_norm_g[0], mla_kv_norm_g[0],
                             wqa, wqb, wk, wv, cos_t, sin_t)
    o = _mla_attn_call(q, k, v)
    xl = _tail_call(xs, o, modv, 2, mla_w_o[0].astype(bf16), ffn_norm_g[2], w_in, w_out, T_LAT)

    l1_t, wc_t, g_t = _fnet_tables()
    tr, ti = _fnet_s1_call(xl.reshape(B, FN_R, FN_R, D), modv, 3, mix_norm_g[3], l1_t)
    o = _fnet_s2_call(tr, ti, wc_t, g_t).reshape(T_LAT, D)
    out = _tail_call(xl, o, modv, 3, fnet_w_o[0].astype(bf16), ffn_norm_g[3], w_in, w_out,
                     T_LAT, final_g=final_norm_g)
    return out.reshape(B, S, D)
```

```python
import functools
import math

import numpy as np
import jax
import jax.numpy as jnp
from jax import lax
from jax.experimental import pallas as pl
from jax.experimental.pallas import tpu as pltpu

D = 1024
B = 4
S = 4096
CTX = 256
DEPTH = 4
GRID_W = 64
N_MOD = 6
F = 2816
EPS = 1e-6

T_LAT = B * S
T_CTX = B * CTX
NT = T_LAT + T_CTX

NA_H = 16
NA_KR = 8
NA_KC = 16
ML_H = 16
ML_RANK = 256
ML_NOPE = 64
ML_ROPE = 32
ML_V = 64
HP = 8
LANE = 128
MASK_NEG = -1e30

TM = 512
TAIL_SUB = 2
TMT = TAIL_SUB * TM
FFN_CK = 256
VMEM_LIMIT = 56 * 1024 * 1024

f32 = jnp.float32
bf16 = jnp.bfloat16


def _resident(shape):
    nd = len(shape)
    return pl.BlockSpec(shape, lambda *_: (0,) * nd, pipeline_mode=pl.Buffered(1))


def _mod_spec_steps(layer):
    return pl.BlockSpec((None, 1, N_MOD * D), lambda t: (layer * 8 + t // (S // TMT), 0, 0))


def _rmsnorm(x, g):
    ms = jnp.mean(x * x, axis=-1, keepdims=True)
    return (x * lax.rsqrt(ms + EPS)) * g


def _norm_mod(x, g, shift, scale):
    return _rmsnorm(x, g) * (1.0 + scale) + shift


def _dot(a, b):
    return jnp.dot(a, b, preferred_element_type=f32)


def _dot_nt(a, b):
    return lax.dot_general(a, b, (((1,), (1,)), ((), ())), preferred_element_type=f32)


def _mod_kernel(cc_ref, w_ref, b_ref, out_ref):
    cc = cc_ref[...]
    s = (cc * jax.nn.sigmoid(cc)).astype(bf16)
    out_ref[...] = _dot(s, w_ref[...].astype(bf16)) + b_ref[...]


def _mod_call(cc8, mod_w, mod_b):
    tn = 1536
    return pl.pallas_call(
        _mod_kernel,
        grid=(DEPTH, N_MOD * D // tn),
        in_specs=[
            pl.BlockSpec((8, D), lambda l, j: (0, 0)),
            pl.BlockSpec((None, D, tn), lambda l, j: (l, 0, j)),
            pl.BlockSpec((None, 1, tn), lambda l, j: (l, 0, j)),
        ],
        out_specs=pl.BlockSpec((None, 8, tn), lambda l, j: (l, 0, j)),
        out_shape=jax.ShapeDtypeStruct((DEPTH, 8, N_MOD * D), f32),
        compiler_params=pltpu.CompilerParams(vmem_limit_bytes=VMEM_LIMIT),
        name="mod_vectors",
    )(cc8, mod_w, mod_b.reshape(DEPTH, 1, N_MOD * D))


def _tile_rows(lat_ref, ctx_ref, n_lat_steps):
    return jnp.where(pl.program_id(0) < n_lat_steps, lat_ref[...], ctx_ref[...])


def _split_specs(block_rows, lat_idx, ctx_idx, n_lat_steps):
    tl = lambda t: jnp.minimum(t, n_lat_steps - 1)
    tc = lambda t: jnp.maximum(t - n_lat_steps, 0)
    return [pl.BlockSpec((block_rows, D), lambda t: (lat_idx(tl(t)), 0)),
            pl.BlockSpec((block_rows, D), lambda t: (ctx_idx(tc(t)), 0))]


def _tail_kernel(*refs, final, split):
    if split:
        xl_ref, xc_ref, *refs = refs
        x = _tile_rows(xl_ref, xc_ref, T_LAT // TMT)
    else:
        x_ref, *refs = refs
        x = x_ref[...]
    o_ref, mod_ref, wo_ref, g_ref, win_ref, wout_ref, *rest = refs
    if final:
        fg_ref, out_ref, acc_ref = rest
    else:
        out_ref, acc_ref = rest
    mod = mod_ref[...]
    g1 = mod[:, 2 * D:3 * D]
    sh2 = mod[:, 3 * D:4 * D]
    sc2 = mod[:, 4 * D:5 * D]
    g2 = mod[:, 5 * D:6 * D]
    tiles = [slice(u * TM, (u + 1) * TM) for u in range(TAIL_SUB)]
    x1 = [x[rows] + g1 * _dot(o_ref[rows, :], wo_ref[...]) for rows in tiles]
    h = [_norm_mod(v, g_ref[...], sh2, sc2).astype(bf16) for v in x1]
    for u in range(TAIL_SUB):
        for c in range(F // FFN_CK):
            lo = c * FFN_CK
            gate = _dot(h[u], win_ref[:, lo:lo + FFN_CK])
            up = _dot(h[u], win_ref[:, F + lo:F + lo + FFN_CK])
            a = (gate * jax.nn.sigmoid(gate) * up).astype(bf16)
            y = _dot(a, wout_ref[lo:lo + FFN_CK, :])
            if c == 0:
                acc_ref[u] = y
            else:
                acc_ref[u] += y
    for u, rows in enumerate(tiles):
        x2 = x1[u] + g2 * acc_ref[u]
        if final:
            x2 = _rmsnorm(x2, fg_ref[...])
        out_ref[rows, :] = x2


def _tail_call(x, o, modv, layer, wo, g, win, wout, n_rows, final_g=None):
    final = final_g is not None
    split = isinstance(x, tuple)
    n_steps = n_rows // TMT
    row = lambda w: pl.BlockSpec((TMT, w), lambda t: (t, 0))
    slab = lambda r, c: pl.BlockSpec((None, r, c), lambda t: (layer, 0, 0), pipeline_mode=pl.Buffered(1))
    x_specs = _split_specs(TMT, lambda i: i, lambda i: i, T_LAT // TMT) if split else [row(D)]
    in_specs = x_specs + [row(D), _mod_spec_steps(layer), _resident((D, D)), _resident((1, D)),
                          slab(D, 2 * F), slab(F, D)]
    args = list(x if split else (x,)) + [o, modv, wo, g.reshape(1, D), win, wout]
    if final:
        in_specs.append(_resident((1, D)))
        args.append(final_g.reshape(1, D))
    return pl.pallas_call(
        functools.partial(_tail_kernel, final=final, split=split),
        grid=(n_steps,),
        in_specs=in_specs,
        out_specs=row(D),
        out_shape=jax.ShapeDtypeStruct((n_rows, D), f32),
        scratch_shapes=[pltpu.VMEM((TAIL_SUB, TM, D), f32)],
        compiler_params=pltpu.CompilerParams(vmem_limit_bytes=VMEM_LIMIT),
        name=f"tail_l{layer}",
    )(*args)


CONV_HALO = 16
CONV_CK = 256


def _conv_kernel(lp_ref, cp_ref, lx_ref, cx_ref, ln_ref, cn_ref, mod_ref, g_ref, win_ref, cw_ref, out_ref):
    t = pl.program_id(0)
    n_lat = T_LAT // TMT
    mod = mod_ref[...]
    sh1 = mod[:, 0:D]
    sc1 = mod[:, D:2 * D]
    g = g_ref[...]
    pieces = [_tile_rows(a, b, n_lat) for a, b in ((lp_ref, cp_ref), (lx_ref, cx_ref), (ln_ref, cn_ref))]
    h_all = jnp.concatenate([_norm_mod(p, g, sh1, sc1).astype(bf16) for p in pieces], axis=0)
    rows = TM + 2 * CONV_HALO
    seq_len = jnp.where(t < n_lat, S, CTX)
    for u in range(TAIL_SUB):
        h = h_all[u * TM:u * TM + rows]
        grow = t * TMT + u * TM - CONV_HALO + lax.broadcasted_iota(jnp.int32, (rows, CONV_CK), 0)
        pos = jnp.bitwise_and(grow, seq_len - 1)
        has_prev = pos != 0
        has_next = pos != seq_len - 1
        for j in range(D // CONV_CK):
            lo = j * CONV_CK
            bg = _dot(h, win_ref[:, lo:lo + CONV_CK])
            cg = _dot(h, win_ref[:, D + lo:D + lo + CONV_CK])
            v = _dot(h, win_ref[:, 2 * D + lo:2 * D + lo + CONV_CK])
            uu = cg * v
            u_prev = jnp.where(has_prev, pltpu.roll(uu, 1, axis=0), 0.0)
            u_next = jnp.where(has_next, pltpu.roll(uu, rows - 1, axis=0), 0.0)
            z = (cw_ref[0:1, lo:lo + CONV_CK] * u_prev + cw_ref[1:2, lo:lo + CONV_CK] * uu
                 + cw_ref[2:3, lo:lo + CONV_CK] * u_next)
            bz = bg * z
            out_ref[u * TM:(u + 1) * TM, lo:lo + CONV_CK] = bz[CONV_HALO:CONV_HALO + TM].astype(bf16)


def _conv_call(x_lat, x_ctx, modv, layer, g, win, cw):
    hb = TMT // CONV_HALO
    n_lat = T_LAT // TMT
    prev_idx = lambda i: jnp.maximum(i * hb - 1, 0)
    next_idx = lambda rows: (lambda i: jnp.minimum((i + 1) * hb, rows // CONV_HALO - 1))
    prev_l, prev_c = _split_specs(CONV_HALO, prev_idx, prev_idx, n_lat)
    next_l, next_c = _split_specs(CONV_HALO, next_idx(T_LAT), next_idx(T_CTX), n_lat)
    main_l, main_c = _split_specs(TMT, lambda i: i, lambda i: i, n_lat)
    return pl.pallas_call(
        _conv_kernel,
        grid=(NT // TMT,),
        in_specs=[prev_l, prev_c, main_l, main_c, next_l, next_c,
                  _mod_spec_steps(layer), _resident((1, D)), _resident((D, 3 * D)), _resident((8, D))],
        out_specs=pl.BlockSpec((TMT, D), lambda t: (t, 0)),
        out_shape=jax.ShapeDtypeStruct((NT, D), bf16),
        compiler_params=pltpu.CompilerParams(vmem_limit_bytes=VMEM_LIMIT),
        name="conv_mixer",
    )(x_lat, x_ctx, x_lat, x_ctx, x_lat, x_ctx, modv, g.reshape(1, D), win, cw)


QKV_CK = 512
NA_SCALE = 0.125


def _qkv_kernel(x_ref, mod_ref, g_ref, w_ref, out_ref):
    mod = mod_ref[...]
    tiles = [slice(u * TM, (u + 1) * TM) for u in range(TAIL_SUB)]
    hs = [_norm_mod(x_ref[rows, :], g_ref[...], mod[:, 0:D], mod[:, D:2 * D]).astype(bf16) for rows in tiles]
    for h, rows in zip(hs, tiles):
        for c in range(3 * D // QKV_CK):
            lo = c * QKV_CK
            r = _dot(h, w_ref[:, lo:lo + QKV_CK])
            if lo < D:
                r = r * NA_SCALE
            out_ref[rows, lo:lo + QKV_CK] = r.astype(bf16)


def _qkv_call(x, modv, layer, g, w):
    return pl.pallas_call(
        _qkv_kernel,
        grid=(NT // TMT,),
        in_specs=[pl.BlockSpec((TMT, D), lambda t: (t, 0)), _mod_spec_steps(layer),
                  _resident((1, D)), _resident((D, 3 * D))],
        out_specs=pl.BlockSpec((TMT, 3 * D), lambda t: (t, 0)),
        out_shape=jax.ShapeDtypeStruct((NT, 3 * D), bf16),
        compiler_params=pltpu.CompilerParams(vmem_limit_bytes=VMEM_LIMIT),
        name="nat_qkv",
    )(x, modv, g.reshape(1, D), w)


def _stack_pair(q2):
    lane = lax.broadcasted_iota(jnp.int32, q2.shape, 1)
    zero = jnp.zeros_like(q2)
    return jnp.concatenate([jnp.where(lane < 64, q2, zero), jnp.where(lane >= 64, q2, zero)], axis=0)


def _unstack_pair(o, n):
    lane = lax.broadcasted_iota(jnp.int32, (n, LANE), 1)
    return jnp.where(lane < 64, o[:n], o[n:])


NA_ROWS = S // GRID_W
NA_SUB = 4
NA_WIN_STEPS = NA_ROWS // NA_SUB
NA_CTX_STEPS = CTX // (NA_SUB * GRID_W)


def _nat_kernel(q_ref, kb_ref, vb_ref, kc_ref, vc_ref, bias_ref, out_ref):
    step = pl.program_id(1)
    lanes = [slice(hp * LANE, (hp + 1) * LANE) for hp in range(HP)]
    nband = NA_KR * GRID_W

    def attend(with_window):
        scores = []
        for sub in range(NA_SUB):
            rows = slice(sub * GRID_W, (sub + 1) * GRID_W)
            r = step * NA_SUB + sub
            r0 = jnp.clip(r - NA_KR // 2, 0, NA_ROWS - NA_KR)
            off = pl.multiple_of(r0 * GRID_W, GRID_W)
            for hp, ls in enumerate(lanes):
                qst = _stack_pair(q_ref[rows, ls])
                s_c = _dot_nt(qst, kc_ref[:, ls])
                s_w = None
                if with_window:
                    s_w = _dot_nt(qst, kb_ref[pl.ds(off, nband), ls]) + bias_ref[r0 - r + NA_KR - 1, hp]
                scores.append((s_w, s_c, off, rows, ls))
        probs = []
        for s_w, s_c, off, rows, ls in scores:
            m = jnp.max(s_c, axis=-1, keepdims=True)
            if with_window:
                m = jnp.maximum(m, jnp.max(s_w, axis=-1, keepdims=True))
            p_c = jnp.exp(s_c - m)
            l = jnp.sum(p_c, axis=-1, keepdims=True)
            p_w = None
            if with_window:
                p_w = jnp.exp(s_w - m)
                l = l + jnp.sum(p_w, axis=-1, keepdims=True)
                p_w = p_w.astype(bf16)
            probs.append((p_w, p_c.astype(bf16), l, off, rows, ls))
        for p_w, p_c, l, off, rows, ls in probs:
            o = _dot(p_c, vc_ref[:, ls])
            if with_window:
                o = o + _dot(p_w, vb_ref[pl.ds(off, nband), ls])
            out_ref[rows, ls] = _unstack_pair(o / l, GRID_W).astype(bf16)

    @pl.when(step < NA_WIN_STEPS)
    def _():
        attend(True)

    @pl.when(step >= NA_WIN_STEPS)
    def _():
        attend(False)


def _nat_call(qkv, bias):
    tq = NA_SUB * GRID_W

    def q_idx(b, s):
        ctx_blk = T_LAT // tq + b * NA_CTX_STEPS + (s - NA_WIN_STEPS)
        return (jnp.where(s < NA_WIN_STEPS, b * NA_WIN_STEPS + s, ctx_blk), 0)

    once = dict(pipeline_mode=pl.Buffered(1))
    ctx_blk = T_LAT // CTX
    return pl.pallas_call(
        _nat_kernel,
        grid=(B, NA_WIN_STEPS + NA_CTX_STEPS),
        in_specs=[
            pl.BlockSpec((tq, D), q_idx),
            pl.BlockSpec((S, D), lambda b, s: (b, 1), **once),
            pl.BlockSpec((S, D), lambda b, s: (b, 2), **once),
            pl.BlockSpec((CTX, D), lambda b, s: (ctx_blk + b, 1)),
            pl.BlockSpec((CTX, D), lambda b, s: (ctx_blk + b, 2)),
            _resident((NA_KR, HP, 2 * GRID_W, NA_KR * GRID_W)),
        ],
        out_specs=pl.BlockSpec((tq, D), q_idx),
        out_shape=jax.ShapeDtypeStruct((NT, D), bf16),
        compiler_params=pltpu.CompilerParams(vmem_limit_bytes=VMEM_LIMIT),
        name="nat_attn",
    )(qkv, qkv, qkv, qkv, qkv, bias)


def _nat_bias_kernel(rpb_ref, out_ref):
    d0 = pl.program_id(0)
    q = lax.broadcasted_iota(jnp.int32, (GRID_W, LANE), 0)
    lane = lax.broadcasted_iota(jnp.int32, (GRID_W, LANE), 1)
    j = lane % GRID_W
    c0 = jnp.clip(q - NA_KC // 2, 0, GRID_W - NA_KC)
    valid = (j >= c0) & (j < c0 + NA_KC)
    for h in range(NA_H):
        hp, e = divmod(h, 2)
        for m in range(NA_KR // 2):
            halves = []
            for u in range(2):
                row = jnp.broadcast_to(rpb_ref[h, pl.ds(d0 + 2 * m + u, 1), :], (GRID_W, LANE))
                shift = (LANE - (NA_KC - 1) + u * GRID_W) % LANE
                halves.append(pltpu.roll(row, shift, axis=1, stride=1, stride_axis=0))
            t = jnp.where(lane < GRID_W, halves[0], halves[1])
            out_ref[hp, e * GRID_W:(e + 1) * GRID_W, m * LANE:(m + 1) * LANE] = jnp.where(valid, t, MASK_NEG)


def _nat_bias_table(rpb):
    rows, taps = 2 * NA_KR - 1, 2 * NA_KC - 1
    rp = jnp.pad(rpb, ((0, 0), (0, 2 * NA_KR - rows), (0, LANE - taps)))
    return pl.pallas_call(
        _nat_bias_kernel,
        grid=(NA_KR,),
        in_specs=[pl.BlockSpec((NA_H, 2 * NA_KR, LANE), lambda d: (0, 0, 0))],
        out_specs=pl.BlockSpec((None, HP, 2 * GRID_W, NA_KR * GRID_W), lambda d: (d, 0, 0, 0)),
        out_shape=jax.ShapeDtypeStruct((NA_KR, HP, 2 * GRID_W, NA_KR * GRID_W), f32),
        compiler_params=pltpu.CompilerParams(vmem_limit_bytes=VMEM_LIMIT),
        name="nat_bias",
    )(rp)


ML_SCALE = (ML_NOPE + ML_ROPE) ** -0.5 * math.log2(math.e)
ML_QW = ML_H * LANE
ML_P1 = 2 * ML_RANK + 2 * LANE
ML_CK = 256


def _mla_proj_kernel(x_ref, mod_ref, g_ref, w1_ref, qg_ref, kvg_ref, wqa_ref, wqb_ref, wk_ref, wv_ref,
                     cos_ref, sin_ref, q_out, k_out, v_out):
    mod = mod_ref[...]
    tiles = [slice(u * TM, (u + 1) * TM) for u in range(TAIL_SUB)]
    lat = []
    for rows in tiles:
        h = _norm_mod(x_ref[rows, :], g_ref[...], mod[:, 0:D], mod[:, D:2 * D]).astype(bf16)
        p = _dot(h, w1_ref[...])
        cq = _rmsnorm(p[:, 0:ML_RANK], qg_ref[...]).astype(bf16)
        ckv = _rmsnorm(p[:, ML_RANK:2 * ML_RANK], kvg_ref[...]).astype(bf16)
        cos = cos_ref[rows, :]
        sin = sin_ref[rows, :]
        krope = p[:, 2 * ML_RANK:2 * ML_RANK + LANE] * cos + p[:, 2 * ML_RANK + LANE:] * sin
        lat.append((cq, ckv, cos, sin, krope))
    ones_lane = lax.broadcasted_iota(jnp.int32, (TM, ML_CK), 1) % LANE >= ML_V
    for (cq, ckv, cos, sin, krope), rows in zip(lat, tiles):
        for c in range(ML_QW // ML_CK):
            lo = c * ML_CK
            qa = _dot(cq, wqa_ref[:, lo:lo + ML_CK])
            qb = _dot(cq, wqb_ref[:, lo:lo + ML_CK])
            kn = _dot(ckv, wk_ref[:, lo:lo + ML_CK])
            for e in range(ML_CK // LANE):
                ls = slice(e * LANE, (e + 1) * LANE)
                q_out[rows, lo + e * LANE:lo + (e + 1) * LANE] = (
                    (qa[:, ls] * cos + qb[:, ls] * sin) * ML_SCALE).astype(bf16)
                k_out[rows, lo + e * LANE:lo + (e + 1) * LANE] = (kn[:, ls] + krope).astype(bf16)
        for c in range(ML_QW // ML_CK):
            lo = c * ML_CK
            vv = _dot(ckv, wv_ref[:, lo:lo + ML_CK])
            v_out[rows, lo:lo + ML_CK] = jnp.where(ones_lane, 1.0, vv).astype(bf16)


def _mla_proj_call(x, modv, layer, g, w1, qg, kvg, wqa, wqb, wk, wv, cos_t, sin_t):
    lat_steps, steps_per_seq = T_LAT // TMT, S // TMT
    rope_idx = lambda t: (jnp.where(t < lat_steps, t % steps_per_seq, steps_per_seq), 0)
    row = lambda w: pl.BlockSpec((TMT, w), lambda t: (t, 0))
    return pl.pallas_call(
        _mla_proj_kernel,
        grid=(NT // TMT,),
        in_specs=[row(D), _mod_spec_steps(layer), _resident((1, D)), _resident((D, ML_P1)),
                  _resident((1, ML_RANK)), _resident((1, ML_RANK)),
                  _resident((ML_RANK, ML_QW)), _resident((ML_RANK, ML_QW)),
                  _resident((ML_RANK, ML_QW)), _resident((ML_RANK, ML_QW)),
                  pl.BlockSpec((TMT, LANE), rope_idx), pl.BlockSpec((TMT, LANE), rope_idx)],
        out_specs=[row(ML_QW), row(ML_QW), row(ML_QW)],
        out_shape=[jax.ShapeDtypeStruct((NT, ML_QW), bf16)] * 3,
        compiler_params=pltpu.CompilerParams(vmem_limit_bytes=VMEM_LIMIT),
        name="mla_proj",
    )(x, modv, g.reshape(1, D), w1, qg.reshape(1, ML_RANK), kvg.reshape(1, ML_RANK),
      wqa, wqb, wk, wv, cos_t, sin_t)


ML_TQ = 1024
ML_EDGES = (0, 2048, S)


def _mla_attn_kernel(q_ref, kl_ref, kc_ref, vl_ref, vc_ref, out_ref):
    qs = [q_ref[:, e * LANE:(e + 1) * LANE] for e in range(2)]
    n_lat = len(ML_EDGES) - 1

    def chunk(lat_ref, ctx_ref, c):
        lat = lat_ref[ML_EDGES[c]:ML_EDGES[c + 1], :]
        return lat if c < n_lat - 1 else jnp.concatenate([lat, ctx_ref[...]], axis=0)

    def keys(c):
        return chunk(kl_ref, kc_ref, c)

    def values(c):
        return chunk(vl_ref, vc_ref, c)

    def update(e, c, carry):
        m, acc = carry
        ls = slice(e * LANE, (e + 1) * LANE)
        s = _dot_nt(qs[e], keys(c)[:, ls])
        m_new = jnp.maximum(m, jnp.max(s, axis=-1, keepdims=True))
        alpha = jnp.exp2(m - m_new)
        p = jnp.exp2(s - m_new)
        acc = alpha * acc + _dot(p.astype(bf16), values(c)[:, ls])
        return m_new, acc

    init = (jnp.full((ML_TQ, 1), -jnp.inf, f32), jnp.zeros((ML_TQ, LANE), f32))
    carry = [init, init]
    for c in range(n_lat):
        carry = [update(e, c, carry[e]) for e in range(2)]
    a0, a1 = carry[0][1], carry[1][1]
    lane = lax.broadcasted_iota(jnp.int32, (ML_TQ, LANE), 1)
    num = jnp.where(lane < ML_V, a0, pltpu.roll(a1, ML_V, axis=1))
    den = jnp.where(lane < ML_V, pltpu.roll(a0, ML_V, axis=1), a1)
    out_ref[...] = (num / den).astype(bf16)


def _mla_attn_call(q, k, v):
    nq = S // ML_TQ
    ctx_blk = T_LAT // CTX
    return pl.pallas_call(
        _mla_attn_kernel,
        grid=(B, HP, nq),
        in_specs=[
            pl.BlockSpec((ML_TQ, 2 * LANE), lambda b, hp, i: (b * nq + i, hp)),
            pl.BlockSpec((S, 2 * LANE), lambda b, hp, i: (b, hp)),
            pl.BlockSpec((CTX, 2 * LANE), lambda b, hp, i: (ctx_blk + b, hp)),
            pl.BlockSpec((S, 2 * LANE), lambda b, hp, i: (b, hp)),
            pl.BlockSpec((CTX, 2 * LANE), lambda b, hp, i: (ctx_blk + b, hp)),
        ],
        out_specs=pl.BlockSpec((ML_TQ, LANE), lambda b, hp, i: (b * nq + i, hp)),
        out_shape=jax.ShapeDtypeStruct((T_LAT, D), bf16),
        compiler_params=pltpu.CompilerParams(vmem_limit_bytes=VMEM_LIMIT),
        name="mla_attn",
    )(q, k, k, v, v)


def _mla_weights(w_dq, w_uq, w_dkv, w_ukv):
    z = lambda *s: jnp.zeros(s, f32)
    uq = w_uq.reshape(ML_RANK, ML_H, ML_NOPE + ML_ROPE)
    nope, r1, r2 = uq[..., :ML_NOPE], uq[..., ML_NOPE:ML_NOPE + 16], uq[..., ML_NOPE + 16:]
    pad = z(ML_RANK, ML_H, LANE - ML_NOPE - ML_ROPE)
    wqa = jnp.concatenate([nope, r1, r2, pad], axis=-1).reshape(ML_RANK, ML_QW)
    wqb = jnp.concatenate([jnp.zeros_like(nope), -r2, r1, pad], axis=-1).reshape(ML_RANK, ML_QW)
    ukv = w_ukv.reshape(ML_RANK, ML_H, ML_NOPE + ML_V)
    wk = jnp.concatenate([ukv[..., :ML_NOPE], z(ML_RANK, ML_H, LANE - ML_NOPE)], axis=-1).reshape(ML_RANK, ML_QW)
    wv = jnp.concatenate([ukv[..., ML_NOPE:], z(ML_RANK, ML_H, LANE - ML_V)], axis=-1).reshape(ML_RANK, ML_QW)
    wr = w_dkv[:, ML_RANK:]
    rope1 = jnp.concatenate([z(D, ML_NOPE), wr, z(D, LANE - ML_NOPE - ML_ROPE)], axis=-1)
    rope2 = jnp.concatenate([z(D, ML_NOPE), -wr[:, 16:], wr[:, :16], z(D, LANE - ML_NOPE - ML_ROPE)], axis=-1)
    w1 = jnp.concatenate([w_dq, w_dkv[:, :ML_RANK], rope1, rope2], axis=-1)
    return tuple(w.astype(bf16) for w in (w1, wqa, wqb, wk, wv))


def _rope_tables():
    n_freq = ML_ROPE // 4
    freq = 10000.0 ** (-jnp.arange(n_freq, dtype=f32) / n_freq)
    t = jnp.arange(S)
    row = (t // GRID_W).astype(f32)
    col = (t % GRID_W).astype(f32)
    ang = jnp.concatenate([row[:, None] * freq, col[:, None] * freq], axis=-1)
    cos, sin = jnp.cos(ang), jnp.sin(ang)
    pad = jnp.zeros((S, LANE - ML_NOPE - ML_ROPE), f32)
    cos_l = jnp.concatenate([jnp.ones((S, ML_NOPE), f32), cos, cos, pad], axis=-1)
    sin_l = jnp.concatenate([jnp.zeros((S, ML_NOPE), f32), sin, sin, pad], axis=-1)
    cos_c = jnp.concatenate([jnp.ones((TMT, ML_NOPE + ML_ROPE), f32), pad[:TMT]], axis=-1)
    return jnp.concatenate([cos_l, cos_c], axis=0), jnp.concatenate([sin_l, jnp.zeros((TMT, LANE), f32)], axis=0)


FN_G = 4
FN_C = D // FN_G
FN_R = 64
FN_J = 16
assert FN_R * FN_R == S


def _regroup(x, outer, inner):
    return x.reshape(outer, inner, x.shape[-1]).swapaxes(0, 1).reshape(x.shape)


def _fnet_s1_kernel(x_ref, mod_ref, g_ref, l1_ref, tr_out, ti_out):
    mod = mod_ref[...]
    x = x_ref[...].reshape(FN_R * FN_J, D)
    h = _norm_mod(x, g_ref[...], mod[:, 0:D], mod[:, D:2 * D])
    h = _regroup(h, FN_R, FN_J).astype(bf16)
    t = jnp.concatenate([_dot(l1_ref[...], h[j * FN_R:(j + 1) * FN_R]) for j in range(FN_J)], axis=0)
    t = _regroup(t, FN_J, 2 * FN_R)
    half = FN_R * FN_J
    tr_out[...] = t[:half].astype(bf16).reshape(FN_R, FN_J, D)
    ti_out[...] = t[half:].astype(bf16).reshape(FN_R, FN_J, D)


def _fnet_s1_call(x4, modv, layer, g, l1):
    blk = pl.BlockSpec((None, FN_R, FN_J, D), lambda b, jb: (b, 0, jb, 0))
    t_shape = jax.ShapeDtypeStruct((B, FN_R, FN_R, D), bf16)
    return pl.pallas_call(
        _fnet_s1_kernel,
        grid=(B, FN_R // FN_J),
        in_specs=[blk, pl.BlockSpec((None, 1, N_MOD * D), lambda b, jb: (layer * 8 + b, 0, 0)),
                  _resident((1, D)), _resident((2 * FN_R, FN_R))],
        out_specs=[blk, blk],
        out_shape=[t_shape, t_shape],
        compiler_params=pltpu.CompilerParams(vmem_limit_bytes=VMEM_LIMIT),
        name="fnet_stage1",
    )(x4, modv, g.reshape(1, D), l1)


def _fnet_s2_kernel(tr_ref, ti_ref, wc_ref, g_ref, out_ref):
    kb = pl.program_id(1)
    tr = tr_ref[...].reshape(FN_J * FN_R, D)
    ti = ti_ref[...].reshape(FN_J * FN_R, D)
    ur, ui = [], []
    for gi in range(FN_G):
        ls = slice(gi * FN_C, (gi + 1) * FN_C)
        u = _dot(jnp.concatenate([tr[:, ls], ti[:, ls]], axis=1), wc_ref[...])
        ur.append(u[:, :FN_C].astype(bf16))
        ui.append(u[:, FN_C:].astype(bf16))
    ur = jnp.concatenate(ur, axis=1)
    ui = jnp.concatenate(ui, axis=1)
    ys = []
    for j in range(FN_J):
        rows = slice(j * FN_R, (j + 1) * FN_R)
        t = jnp.concatenate([ur[rows], ui[rows]], axis=0)
        ys.append(_dot(g_ref[kb * FN_J + j], t))
    y = _regroup(jnp.concatenate(ys, axis=0), FN_J, FN_R)
    out_ref[...] = (y * (1.0 / math.sqrt(S * FN_C))).astype(bf16).reshape(FN_R, FN_J, D)


def _fnet_s2_call(tr, ti, wc, gtab):
    t_blk = pl.BlockSpec((None, FN_J, FN_R, D), lambda b, kb: (b, kb, 0, 0))
    return pl.pallas_call(
        _fnet_s2_kernel,
        grid=(B, FN_R // FN_J),
        in_specs=[t_blk, t_blk, _resident((2 * FN_C, 2 * FN_C)), _resident((FN_R, FN_R, 2 * FN_R))],
        out_specs=pl.BlockSpec((None, FN_R, FN_J, D), lambda b, kb: (b, 0, kb, 0)),
        out_shape=jax.ShapeDtypeStruct((B, FN_R, FN_R, D), bf16),
        compiler_params=pltpu.CompilerParams(vmem_limit_bytes=VMEM_LIMIT),
        name="fnet_stage2",
    )(tr, ti, wc, gtab)


def _cos_sin(num, den):
    ang = (np.asarray(num) % den).astype(np.float64) * (2.0 * np.pi / den)
    return np.cos(ang), np.sin(ang)


def _fnet_tables():
    ch = np.arange(FN_C)
    cc, sc = _cos_sin(ch[:, None] * ch[None, :], FN_C)
    r = np.arange(FN_R)
    c1, s1 = _cos_sin(r[:, None] * r[None, :], FN_R)
    l1 = np.concatenate([c1, -s1], axis=0)
    wc = np.block([[cc, -sc], [sc, cc]])
    k = r[:, None, None] + FN_R * r[None, :, None]
    gc, gs = _cos_sin(k * r[None, None, :], S)
    gtab = np.concatenate([gc, gs], axis=-1)
    return tuple(jnp.asarray(t, dtype=f32).astype(bf16) for t in (l1, wc, gtab))


def kernel(x, c, ctx, c_ctx, mod_w, mod_b, mix_norm_g, ffn_norm_g, conv_w_in, conv_w, conv_w_out,
           nat_w_qkv, nat_rpb, nat_w_o, mla_w_dq, mla_q_norm_g, mla_w_uq, mla_w_dkv, mla_kv_norm_g,
           mla_w_ukv, mla_w_o, fnet_w_o, ffn_w_in, ffn_w_out, final_norm_g):
    cc8 = jnp.concatenate([c, c_ctx[None, :], jnp.zeros((8 - B - 1, D), f32)], axis=0)
    modv = _mod_call(cc8, mod_w, mod_b).reshape(DEPTH * 8, 1, N_MOD * D)

    w_in = ffn_w_in.astype(bf16)
    w_out = ffn_w_out.astype(bf16)
    x_lat = x.reshape(T_LAT, D)
    x_ctx = ctx.reshape(T_CTX, D)

    cw8 = jnp.concatenate([conv_w[0], jnp.zeros((5, D), f32)], axis=0)
    bz = _conv_call(x_lat, x_ctx, modv, 0, mix_norm_g[0], conv_w_in[0].astype(bf16), cw8)
    xs = _tail_call((x_lat, x_ctx), bz, modv, 0, conv_w_out[0].astype(bf16), ffn_norm_g[0], w_in, w_out,
                    NT)

    qkv = _qkv_call(xs, modv, 1, mix_norm_g[1], nat_w_qkv[0].astype(bf16))
    o = _nat_call(qkv, _nat_bias_table(nat_rpb[0]))
    xs = _tail_call(xs, o, modv, 1, nat_w_o[0].astype(bf16), ffn_norm_g[1], w_in, w_out, NT)

    w1, wqa, wqb, wk, wv = _mla_weights(mla_w_dq[0], mla_w_uq[0], mla_w_dkv[0], mla_w_ukv[0])
    cos_t, sin_t = _rope_tables()
    q, k, v = _mla_proj_call(xs, modv, 2, mix_norm_g[2], w1, mla_q_norm_g[0], mla_kv_norm_g[0],
                             wqa, wqb, wk, wv, cos_t, sin_t)
    o = _mla_attn_call(q, k, v)
    xl = _tail_call(xs, o, modv, 2, mla_w_o[0].astype(bf16), ffn_norm_g[2], w_in, w_out, T_LAT)

    l1_t, wc_t, g_t = _fnet_tables()
    tr, ti = _fnet_s1_call(xl.reshape(B, FN_R, FN_R, D), modv, 3, mix_norm_g[3], l1_t)
    o = _fnet_s2_call(tr, ti, wc_t, g_t).reshape(T_LAT, D)
    out = _tail_call(xl, o, modv, 3, fnet_w_o[0].astype(bf16), ffn_norm_g[3], w_in, w_out,
                     T_LAT, final_g=final_norm_g)
    return out.reshape(B, S, D)
```

```python
import functools
import math

import numpy as np
import jax
import jax.numpy as jnp
from jax import lax
from jax.experimental import pallas as pl
from jax.experimental.pallas import tpu as pltpu

D = 1024
B = 4
S = 4096
CTX = 256
DEPTH = 4
GRID_W = 64
N_MOD = 6
F = 2816
EPS = 1e-6

T_LAT = B * S
T_CTX = B * CTX
NT = T_LAT + T_CTX

NA_H = 16
NA_KR = 8
NA_KC = 16
ML_H = 16
ML_RANK = 256
ML_NOPE = 64
ML_ROPE = 32
ML_V = 64
HP = 8
LANE = 128
MASK_NEG = -1e30

TM = 512
TAIL_SUB = 2
TMT = TAIL_SUB * TM
FFN_CK = 256
VMEM_LIMIT = 56 * 1024 * 1024

f32 = jnp.float32
bf16 = jnp.bfloat16


def _resident(shape):
    nd = len(shape)
    return pl.BlockSpec(shape, lambda *_: (0,) * nd, pipeline_mode=pl.Buffered(1))


def _mod_spec_steps(layer):
    return pl.BlockSpec((None, 1, N_MOD * D), lambda t: (layer * 8 + t // (S // TMT), 0, 0))


def _rmsnorm(x, g):
    ms = jnp.mean(x * x, axis=-1, keepdims=True)
    return (x * lax.rsqrt(ms + EPS)) * g


def _norm_mod(x, g, shift, scale):
    return _rmsnorm(x, g) * (1.0 + scale) + shift


def _dot(a, b):
    return jnp.dot(a, b, preferred_element_type=f32)


def _dot_nt(a, b):
    return lax.dot_general(a, b, (((1,), (1,)), ((), ())), preferred_element_type=f32)


def _mod_kernel(cc_ref, w_ref, b_ref, out_ref):
    cc = cc_ref[...]
    s = (cc * jax.nn.sigmoid(cc)).astype(bf16)
    out_ref[...] = _dot(s, w_ref[...].astype(bf16)) + b_ref[...]


def _mod_call(cc8, mod_w, mod_b):
    tn = 1536
    return pl.pallas_call(
        _mod_kernel,
        grid=(DEPTH, N_MOD * D // tn),
        in_specs=[
            pl.BlockSpec((8, D), lambda l, j: (0, 0)),
            pl.BlockSpec((None, D, tn), lambda l, j: (l, 0, j)),
            pl.BlockSpec((None, 1, tn), lambda l, j: (l, 0, j)),
        ],
        out_specs=pl.BlockSpec((None, 8, tn), lambda l, j: (l, 0, j)),
        out_shape=jax.ShapeDtypeStruct((DEPTH, 8, N_MOD * D), f32),
        compiler_params=pltpu.CompilerParams(vmem_limit_bytes=VMEM_LIMIT),
        name="mod_vectors",
    )(cc8, mod_w, mod_b.reshape(DEPTH, 1, N_MOD * D))


def _tile_rows(lat_ref, ctx_ref, n_lat_steps):
    return jnp.where(pl.program_id(0) < n_lat_steps, lat_ref[...], ctx_ref[...])


def _split_specs(block_rows, lat_idx, ctx_idx, n_lat_steps):
    tl = lambda t: jnp.minimum(t, n_lat_steps - 1)
    tc = lambda t: jnp.maximum(t - n_lat_steps, 0)
    return [pl.BlockSpec((block_rows, D), lambda t: (lat_idx(tl(t)), 0)),
            pl.BlockSpec((block_rows, D), lambda t: (ctx_idx(tc(t)), 0))]


def _tail_kernel(*refs, final, split):
    if split:
        xl_ref, xc_ref, *refs = refs
        x = _tile_rows(xl_ref, xc_ref, T_LAT // TMT)
    else:
        x_ref, *refs = refs
        x = x_ref[...]
    o_ref, mod_ref, wo_ref, g_ref, win_ref, wout_ref, *rest = refs
    if final:
        fg_ref, out_ref, acc_ref = rest
    else:
        out_ref, acc_ref = rest
    mod = mod_ref[...]
    g1 = mod[:, 2 * D:3 * D]
    sh2 = mod[:, 3 * D:4 * D]
    sc2 = mod[:, 4 * D:5 * D]
    g2 = mod[:, 5 * D:6 * D]
    tiles = [slice(u * TM, (u + 1) * TM) for u in range(TAIL_SUB)]
    x1 = [x[rows] + g1 * _dot(o_ref[rows, :], wo_ref[...]) for rows in tiles]
    h = [_norm_mod(v, g_ref[...], sh2, sc2).astype(bf16) for v in x1]
    for u in range(TAIL_SUB):
        for c in range(F // FFN_CK):
            lo = c * FFN_CK
            gate = _dot(h[u], win_ref[:, lo:lo + FFN_CK])
            up = _dot(h[u], win_ref[:, F + lo:F + lo + FFN_CK])
            a = (gate * jax.nn.sigmoid(gate) * up).astype(bf16)
            y = _dot(a, wout_ref[lo:lo + FFN_CK, :])
            if c == 0:
                acc_ref[u] = y
            else:
                acc_ref[u] += y
    for u, rows in enumerate(tiles):
        x2 = x1[u] + g2 * acc_ref[u]
        if final:
            x2 = _rmsnorm(x2, fg_ref[...])
        out_ref[rows, :] = x2


def _tail_call(x, o, modv, layer, wo, g, win, wout, n_rows, final_g=None):
    final = final_g is not None
    split = isinstance(x, tuple)
    n_steps = n_rows // TMT
    row = lambda w: pl.BlockSpec((TMT, w), lambda t: (t, 0))
    slab = lambda r, c: pl.BlockSpec((None, r, c), lambda t: (layer, 0, 0), pipeline_mode=pl.Buffered(1))
    x_specs = _split_specs(TMT, lambda i: i, lambda i: i, T_LAT // TMT) if split else [row(D)]
    in_specs = x_specs + [row(D), _mod_spec_steps(layer), _resident((D, D)), _resident((1, D)),
                          slab(D, 2 * F), slab(F, D)]
    args = list(x if split else (x,)) + [o, modv, wo, g.reshape(1, D), win, wout]
    if final:
        in_specs.append(_resident((1, D)))
        args.append(final_g.reshape(1, D))
    return pl.pallas_call(
        functools.partial(_tail_kernel, final=final, split=split),
        grid=(n_steps,),
        in_specs=in_specs,
        out_specs=row(D),
        out_shape=jax.ShapeDtypeStruct((n_rows, D), f32),
        scratch_shapes=[pltpu.VMEM((TAIL_SUB, TM, D), f32)],
        compiler_params=pltpu.CompilerParams(vmem_limit_bytes=VMEM_LIMIT),
        name=f"tail_l{layer}",
    )(*args)


CONV_HALO = 16
CONV_CK = 256


def _conv_kernel(lp_ref, cp_ref, lx_ref, cx_ref, ln_ref, cn_ref, mod_ref, g_ref, win_ref, cw_ref, out_ref):
    t = pl.program_id(0)
    n_lat = T_LAT // TMT
    mod = mod_ref[...]
    sh1 = mod[:, 0:D]
    sc1 = mod[:, D:2 * D]
    g = g_ref[...]
    pieces = [_tile_rows(a, b, n_lat) for a, b in ((lp_ref, cp_ref), (lx_ref, cx_ref), (ln_ref, cn_ref))]
    h_all = jnp.concatenate([_norm_mod(p, g, sh1, sc1).astype(bf16) for p in pieces], axis=0)
    rows = TM + 2 * CONV_HALO
    seq_len = jnp.where(t < n_lat, S, CTX)
    for u in range(TAIL_SUB):
        h = h_all[u * TM:u * TM + rows]
        grow = t * TMT + u * TM - CONV_HALO + lax.broadcasted_iota(jnp.int32, (rows, CONV_CK), 0)
        pos = jnp.bitwise_and(grow, seq_len - 1)
        has_prev = pos != 0
        has_next = pos != seq_len - 1
        for j in range(D // CONV_CK):
            lo = j * CONV_CK
            bg = _dot(h, win_ref[:, lo:lo + CONV_CK])
            cg = _dot(h, win_ref[:, D + lo:D + lo + CONV_CK])
            v = _dot(h, win_ref[:, 2 * D + lo:2 * D + lo + CONV_CK])
            uu = cg * v
            u_prev = jnp.where(has_prev, pltpu.roll(uu, 1, axis=0), 0.0)
            u_next = jnp.where(has_next, pltpu.roll(uu, rows - 1, axis=0), 0.0)
            z = (cw_ref[0:1, lo:lo + CONV_CK] * u_prev + cw_ref[1:2, lo:lo + CONV_CK] * uu
                 + cw_ref[2:3, lo:lo + CONV_CK] * u_next)
            bz = bg * z
            out_ref[u * TM:(u + 1) * TM, lo:lo + CONV_CK] = bz[CONV_HALO:CONV_HALO + TM].astype(bf16)


def _conv_call(x_lat, x_ctx, modv, layer, g, win, cw):
    hb = TMT // CONV_HALO
    n_lat = T_LAT // TMT
    prev_idx = lambda i: jnp.maximum(i * hb - 1, 0)
    next_idx = lambda rows: (lambda i: jnp.minimum((i + 1) * hb, rows // CONV_HALO - 1))
    prev_l, prev_c = _split_specs(CONV_HALO, prev_idx, prev_idx, n_lat)
    next_l, next_c = _split_specs(CONV_HALO, next_idx(T_LAT), next_idx(T_CTX), n_lat)
    main_l, main_c = _split_specs(TMT, lambda i: i, lambda i: i, n_lat)
    return pl.pallas_call(
        _conv_kernel,
        grid=(NT // TMT,),
        in_specs=[prev_l, prev_c, main_l, main_c, next_l, next_c,
                  _mod_spec_steps(layer), _resident((1, D)), _resident((D, 3 * D)), _resident((8, D))],
        out_specs=pl.BlockSpec((TMT, D), lambda t: (t, 0)),
        out_shape=jax.ShapeDtypeStruct((NT, D), bf16),
        compiler_params=pltpu.CompilerParams(vmem_limit_bytes=VMEM_LIMIT),
        name="conv_mixer",
    )(x_lat, x_ctx, x_lat, x_ctx, x_lat, x_ctx, modv, g.reshape(1, D), win, cw)


QKV_CK = 512
NA_SCALE = 0.125


def _qkv_kernel(x_ref, mod_ref, g_ref, w_ref, out_ref):
    mod = mod_ref[...]
    tiles = [slice(u * TM, (u + 1) * TM) for u in range(TAIL_SUB)]
    hs = [_norm_mod(x_ref[rows, :], g_ref[...], mod[:, 0:D], mod[:, D:2 * D]).astype(bf16) for rows in tiles]
    for h, rows in zip(hs, tiles):
        for c in range(3 * D // QKV_CK):
            lo = c * QKV_CK
            r = _dot(h, w_ref[:, lo:lo + QKV_CK])
            if lo < D:
                r = r * NA_SCALE
            out_ref[rows, lo:lo + QKV_CK] = r.astype(bf16)


def _qkv_call(x, modv, layer, g, w):
    return pl.pallas_call(
        _qkv_kernel,
        grid=(NT // TMT,),
        in_specs=[pl.BlockSpec((TMT, D), lambda t: (t, 0)), _mod_spec_steps(layer),
                  _resident((1, D)), _resident((D, 3 * D))],
        out_specs=pl.BlockSpec((TMT, 3 * D), lambda t: (t, 0)),
        out_shape=jax.ShapeDtypeStruct((NT, 3 * D), bf16),
        compiler_params=pltpu.CompilerParams(vmem_limit_bytes=VMEM_LIMIT),
        name="nat_qkv",
    )(x, modv, g.reshape(1, D), w)


def _stack_pair(q2):
    lane = lax.broadcasted_iota(jnp.int32, q2.shape, 1)
    zero = jnp.zeros_like(q2)
    return jnp.concatenate([jnp.where(lane < 64, q2, zero), jnp.where(lane >= 64, q2, zero)], axis=0)


def _unstack_pair(o, n):
    lane = lax.broadcasted_iota(jnp.int32, (n, LANE), 1)
    return jnp.where(lane < 64, o[:n], o[n:])


NA_ROWS = S // GRID_W
NA_SUB = 4
NA_WIN_STEPS = NA_ROWS // NA_SUB
NA_CTX_STEPS = CTX // (NA_SUB * GRID_W)


def _nat_kernel(q_ref, kb_ref, vb_ref, kc_ref, vc_ref, bias_ref, out_ref):
    step = pl.program_id(1)
    lanes = [slice(hp * LANE, (hp + 1) * LANE) for hp in range(HP)]
    nband = NA_KR * GRID_W

    def attend(with_window):
        scores = []
        for sub in range(NA_SUB):
            rows = slice(sub * GRID_W, (sub + 1) * GRID_W)
            r = step * NA_SUB + sub
            r0 = jnp.clip(r - NA_KR // 2, 0, NA_ROWS - NA_KR)
            off = pl.multiple_of(r0 * GRID_W, GRID_W)
            for hp, ls in enumerate(lanes):
                qst = _stack_pair(q_ref[rows, ls])
                s_c = _dot_nt(qst, kc_ref[:, ls])
                s_w = None
                if with_window:
                    s_w = _dot_nt(qst, kb_ref[pl.ds(off, nband), ls]) + bias_ref[r0 - r + NA_KR - 1, hp]
                scores.append((s_w, s_c, off, rows, ls))
        probs = []
        for s_w, s_c, off, rows, ls in scores:
            m = jnp.max(s_c, axis=-1, keepdims=True)
            if with_window:
                m = jnp.maximum(m, jnp.max(s_w, axis=-1, keepdims=True))
            p_c = jnp.exp(s_c - m)
            l = jnp.sum(p_c, axis=-1, keepdims=True)
            p_w = None
            if with_window:
                p_w = jnp.exp(s_w - m)
                l = l + jnp.sum(p_w, axis=-1, keepdims=True)
                p_w = p_w.astype(bf16)
            probs.append((p_w, p_c.astype(bf16), l, off, rows, ls))
        for p_w, p_c, l, off, rows, ls in probs:
            o = _dot(p_c, vc_ref[:, ls])
            if with_window:
                o = o + _dot(p_w, vb_ref[pl.ds(off, nband), ls])
            out_ref[rows, ls] = _unstack_pair(o / l, GRID_W).astype(bf16)

    @pl.when(step < NA_WIN_STEPS)
    def _():
        attend(True)

    @pl.when(step >= NA_WIN_STEPS)
    def _():
        attend(False)


def _nat_call(qkv, bias):
    tq = NA_SUB * GRID_W

    def q_idx(b, s):
        ctx_blk = T_LAT // tq + b * NA_CTX_STEPS + (s - NA_WIN_STEPS)
        return (jnp.where(s < NA_WIN_STEPS, b * NA_WIN_STEPS + s, ctx_blk), 0)

    once = dict(pipeline_mode=pl.Buffered(1))
    ctx_blk = T_LAT // CTX
    return pl.pallas_call(
        _nat_kernel,
        grid=(B, NA_WIN_STEPS + NA_CTX_STEPS),
        in_specs=[
            pl.BlockSpec((tq, D), q_idx),
            pl.BlockSpec((S, D), lambda b, s: (b, 1), **once),
            pl.BlockSpec((S, D), lambda b, s: (b, 2), **once),
            pl.BlockSpec((CTX, D), lambda b, s: (ctx_blk + b, 1)),
            pl.BlockSpec((CTX, D), lambda b, s: (ctx_blk + b, 2)),
            _resident((NA_KR, HP, 2 * GRID_W, NA_KR * GRID_W)),
        ],
        out_specs=pl.BlockSpec((tq, D), q_idx),
        out_shape=jax.ShapeDtypeStruct((NT, D), bf16),
        compiler_params=pltpu.CompilerParams(vmem_limit_bytes=VMEM_LIMIT),
        name="nat_attn",
    )(qkv, qkv, qkv, qkv, qkv, bias)


def _nat_bias_kernel(rpb_ref, out_ref):
    d0 = pl.program_id(0)
    q = lax.broadcasted_iota(jnp.int32, (GRID_W, LANE), 0)
    lane = lax.broadcasted_iota(jnp.int32, (GRID_W, LANE), 1)
    j = lane % GRID_W
    c0 = jnp.clip(q - NA_KC // 2, 0, GRID_W - NA_KC)
    valid = (j >= c0) & (j < c0 + NA_KC)
    for h in range(NA_H):
        hp, e = divmod(h, 2)
        for m in range(NA_KR // 2):
            halves = []
            for u in range(2):
                row = jnp.broadcast_to(rpb_ref[h, pl.ds(d0 + 2 * m + u, 1), :], (GRID_W, LANE))
                shift = (LANE - (NA_KC - 1) + u * GRID_W) % LANE
                halves.append(pltpu.roll(row, shift, axis=1, stride=1, stride_axis=0))
            t = jnp.where(lane < GRID_W, halves[0], halves[1])
            out_ref[hp, e * GRID_W:(e + 1) * GRID_W, m * LANE:(m + 1) * LANE] = jnp.where(valid, t, MASK_NEG)


def _nat_bias_table(rpb):
    rows, taps = 2 * NA_KR - 1, 2 * NA_KC - 1
    rp = jnp.pad(rpb, ((0, 0), (0, 2 * NA_KR - rows), (0, LANE - taps)))
    return pl.pallas_call(
        _nat_bias_kernel,
        grid=(NA_KR,),
        in_specs=[pl.BlockSpec((NA_H, 2 * NA_KR, LANE), lambda d: (0, 0, 0))],
        out_specs=pl.BlockSpec((None, HP, 2 * GRID_W, NA_KR * GRID_W), lambda d: (d, 0, 0, 0)),
        out_shape=jax.ShapeDtypeStruct((NA_KR, HP, 2 * GRID_W, NA_KR * GRID_W), f32),
        compiler_params=pltpu.CompilerParams(vmem_limit_bytes=VMEM_LIMIT),
        name="nat_bias",
    )(rp)


ML_SCALE = (ML_NOPE + ML_ROPE) ** -0.5 * math.log2(math.e)
ML_QW = ML_H * LANE
ML_P1 = 2 * ML_RANK + 2 * LANE
ML_CK = 256


def _mla_proj_kernel(x_ref, mod_ref, g_ref, w1_ref, qg_ref, kvg_ref, wqa_ref, wqb_ref, wk_ref, wv_ref,
                     cos_ref, sin_ref, q_out, k_out, v_out):
    mod = mod_ref[...]
    tiles = [slice(u * TM, (u + 1) * TM) for u in range(TAIL_SUB)]
    lat = []
    for rows in tiles:
        h = _norm_mod(x_ref[rows, :], g_ref[...], mod[:, 0:D], mod[:, D:2 * D]).astype(bf16)
        p = _dot(h, w1_ref[...])
        cq = _rmsnorm(p[:, 0:ML_RANK], qg_ref[...]).astype(bf16)
        ckv = _rmsnorm(p[:, ML_RANK:2 * ML_RANK], kvg_ref[...]).astype(bf16)
        cos = cos_ref[rows, :]
        sin = sin_ref[rows, :]
        krope = p[:, 2 * ML_RANK:2 * ML_RANK + LANE] * cos + p[:, 2 * ML_RANK + LANE:] * sin
        lat.append((cq, ckv, cos, sin, krope))
    ones_lane = lax.broadcasted_iota(jnp.int32, (TM, ML_CK), 1) % LANE >= ML_V
    for (cq, ckv, cos, sin, krope), rows in zip(lat, tiles):
        for c in range(ML_QW // ML_CK):
            lo = c * ML_CK
            qa = _dot(cq, wqa_ref[:, lo:lo + ML_CK])
            qb = _dot(cq, wqb_ref[:, lo:lo + ML_CK])
            kn = _dot(ckv, wk_ref[:, lo:lo + ML_CK])
            for e in range(ML_CK // LANE):
                ls = slice(e * LANE, (e + 1) * LANE)
                q_out[rows, lo + e * LANE:lo + (e + 1) * LANE] = (
                    (qa[:, ls] * cos + qb[:, ls] * sin) * ML_SCALE).astype(bf16)
                k_out[rows, lo + e * LANE:lo + (e + 1) * LANE] = (kn[:, ls] + krope).astype(bf16)
        for c in range(ML_QW // ML_CK):
            lo = c * ML_CK
            vv = _dot(ckv, wv_ref[:, lo:lo + ML_CK])
            v_out[rows, lo:lo + ML_CK] = jnp.where(ones_lane, 1.0, vv).astype(bf16)


def _mla_proj_call(x, modv, layer, g, w1, qg, kvg, wqa, wqb, wk, wv, cos_t, sin_t):
    lat_steps, steps_per_seq = T_LAT // TMT, S // TMT
    rope_idx = lambda t: (jnp.where(t < lat_steps, t % steps_per_seq, steps_per_seq), 0)
    row = lambda w: pl.BlockSpec((TMT, w), lambda t: (t, 0))
    return pl.pallas_call(
        _mla_proj_kernel,
        grid=(NT // TMT,),
        in_specs=[row(D), _mod_spec_steps(layer), _resident((1, D)), _resident((D, ML_P1)),
                  _resident((1, ML_RANK)), _resident((1, ML_RANK)),
                  _resident((ML_RANK, ML_QW)), _resident((ML_RANK, ML_QW)),
                  _resident((ML_RANK, ML_QW)), _resident((ML_RANK, ML_QW)),
                  pl.BlockSpec((TMT, LANE), rope_idx), pl.BlockSpec((TMT, LANE), rope_idx)],
        out_specs=[row(ML_QW), row(ML_QW), row(ML_QW)],
        out_shape=[jax.ShapeDtypeStruct((NT, ML_QW), bf16)] * 3,
        compiler_params=pltpu.CompilerParams(vmem_limit_bytes=VMEM_LIMIT),
        name="mla_proj",
    )(x, modv, g.reshape(1, D), w1, qg.reshape(1, ML_RANK), kvg.reshape(1, ML_RANK),
      wqa, wqb, wk, wv, cos_t, sin_t)


ML_TQ = 1024
ML_EDGES = (0, 2048, S)


def _mla_attn_kernel(q_ref, kl_ref, kc_ref, vl_ref, vc_ref, out_ref):
    qs = [q_ref[:, e * LANE:(e + 1) * LANE] for e in range(2)]
    n_lat = len(ML_EDGES) - 1

    def chunk(lat_ref, ctx_ref, c):
        lat = lat_ref[ML_EDGES[c]:ML_EDGES[c + 1], :]
        return lat if c < n_lat - 1 else jnp.concatenate([lat, ctx_ref[...]], axis=0)

    def keys(c):
        return chunk(kl_ref, kc_ref, c)

    def values(c):
        return chunk(vl_ref, vc_ref, c)

    def update(e, c, carry):
        ls = slice(e * LANE, (e + 1) * LANE)
        s = _dot_nt(qs[e], keys(c)[:, ls])
        m_new = jnp.max(s, axis=-1, keepdims=True)
        if carry is None:
            return m_new, _dot(jnp.exp2(s - m_new).astype(bf16), values(c)[:, ls])
        m, acc = carry
        m_new = jnp.maximum(m, m_new)
        alpha = jnp.exp2(m - m_new)
        p = jnp.exp2(s - m_new)
        acc = alpha * acc + _dot(p.astype(bf16), values(c)[:, ls])
        return m_new, acc

    carry = [None, None]
    for c in range(n_lat):
        carry = [update(e, c, carry[e]) for e in range(2)]
    a0, a1 = carry[0][1], carry[1][1]
    lane = lax.broadcasted_iota(jnp.int32, (ML_TQ, LANE), 1)
    num = jnp.where(lane < ML_V, a0, pltpu.roll(a1, ML_V, axis=1))
    den = jnp.where(lane < ML_V, pltpu.roll(a0, ML_V, axis=1), a1)
    out_ref[...] = (num / den).astype(bf16)


def _mla_attn_call(q, k, v):
    nq = S // ML_TQ
    ctx_blk = T_LAT // CTX
    return pl.pallas_call(
        _mla_attn_kernel,
        grid=(B, HP, nq),
        in_specs=[
            pl.BlockSpec((ML_TQ, 2 * LANE), lambda b, hp, i: (b * nq + i, hp)),
            pl.BlockSpec((S, 2 * LANE), lambda b, hp, i: (b, hp)),
            pl.BlockSpec((CTX, 2 * LANE), lambda b, hp, i: (ctx_blk + b, hp)),
            pl.BlockSpec((S, 2 * LANE), lambda b, hp, i: (b, hp)),
            pl.BlockSpec((CTX, 2 * LANE), lambda b, hp, i: (ctx_blk + b, hp)),
        ],
        out_specs=pl.BlockSpec((ML_TQ, LANE), lambda b, hp, i: (b * nq + i, hp)),
        out_shape=jax.ShapeDtypeStruct((T_LAT, D), bf16),
        compiler_params=pltpu.CompilerParams(vmem_limit_bytes=VMEM_LIMIT),
        name="mla_attn",
    )(q, k, k, v, v)


def _mla_weights(w_dq, w_uq, w_dkv, w_ukv):
    z = lambda *s: jnp.zeros(s, f32)
    uq = w_uq.reshape(ML_RANK, ML_H, ML_NOPE + ML_ROPE)
    nope, r1, r2 = uq[..., :ML_NOPE], uq[..., ML_NOPE:ML_NOPE + 16], uq[..., ML_NOPE + 16:]
    pad = z(ML_RANK, ML_H, LANE - ML_NOPE - ML_ROPE)
    wqa = jnp.concatenate([nope, r1, r2, pad], axis=-1).reshape(ML_RANK, ML_QW)
    wqb = jnp.concatenate([jnp.zeros_like(nope), -r2, r1, pad], axis=-1).reshape(ML_RANK, ML_QW)
    ukv = w_ukv.reshape(ML_RANK, ML_H, ML_NOPE + ML_V)
    wk = jnp.concatenate([ukv[..., :ML_NOPE], z(ML_RANK, ML_H, LANE - ML_NOPE)], axis=-1).reshape(ML_RANK, ML_QW)
    wv = jnp.concatenate([ukv[..., ML_NOPE:], z(ML_RANK, ML_H, LANE - ML_V)], axis=-1).reshape(ML_RANK, ML_QW)
    wr = w_dkv[:, ML_RANK:]
    rope1 = jnp.concatenate([z(D, ML_NOPE), wr, z(D, LANE - ML_NOPE - ML_ROPE)], axis=-1)
    rope2 = jnp.concatenate([z(D, ML_NOPE), -wr[:, 16:], wr[:, :16], z(D, LANE - ML_NOPE - ML_ROPE)], axis=-1)
    w1 = jnp.concatenate([w_dq, w_dkv[:, :ML_RANK], rope1, rope2], axis=-1)
    return tuple(w.astype(bf16) for w in (w1, wqa, wqb, wk, wv))


def _rope_tables():
    n_freq = ML_ROPE // 4
    freq = 10000.0 ** (-jnp.arange(n_freq, dtype=f32) / n_freq)
    t = jnp.arange(S)
    row = (t // GRID_W).astype(f32)
    col = (t % GRID_W).astype(f32)
    ang = jnp.concatenate([row[:, None] * freq, col[:, None] * freq], axis=-1)
    cos, sin = jnp.cos(ang), jnp.sin(ang)
    pad = jnp.zeros((S, LANE - ML_NOPE - ML_ROPE), f32)
    cos_l = jnp.concatenate([jnp.ones((S, ML_NOPE), f32), cos, cos, pad], axis=-1)
    sin_l = jnp.concatenate([jnp.zeros((S, ML_NOPE), f32), sin, sin, pad], axis=-1)
    cos_c = jnp.concatenate([jnp.ones((TMT, ML_NOPE + ML_ROPE), f32), pad[:TMT]], axis=-1)
    return jnp.concatenate([cos_l, cos_c], axis=0), jnp.concatenate([sin_l, jnp.zeros((TMT, LANE), f32)], axis=0)


FN_G = 4
FN_C = D // FN_G
FN_R = 64
FN_J = 16
assert FN_R * FN_R == S


def _regroup(x, outer, inner):
    return x.reshape(outer, inner, x.shape[-1]).swapaxes(0, 1).reshape(x.shape)


def _fnet_s1_kernel(x_ref, mod_ref, g_ref, l1_ref, tr_out, ti_out):
    mod = mod_ref[...]
    x = x_ref[...].reshape(FN_R * FN_J, D)
    h = _norm_mod(x, g_ref[...], mod[:, 0:D], mod[:, D:2 * D])
    h = _regroup(h, FN_R, FN_J).astype(bf16)
    t = jnp.concatenate([_dot(l1_ref[...], h[j * FN_R:(j + 1) * FN_R]) for j in range(FN_J)], axis=0)
    t = _regroup(t, FN_J, 2 * FN_R)
    half = FN_R * FN_J
    tr_out[...] = t[:half].astype(bf16).reshape(FN_R, FN_J, D)
    ti_out[...] = t[half:].astype(bf16).reshape(FN_R, FN_J, D)


def _fnet_s1_call(x4, modv, layer, g, l1):
    blk = pl.BlockSpec((None, FN_R, FN_J, D), lambda b, jb: (b, 0, jb, 0))
    t_shape = jax.ShapeDtypeStruct((B, FN_R, FN_R, D), bf16)
    return pl.pallas_call(
        _fnet_s1_kernel,
        grid=(B, FN_R // FN_J),
        in_specs=[blk, pl.BlockSpec((None, 1, N_MOD * D), lambda b, jb: (layer * 8 + b, 0, 0)),
                  _resident((1, D)), _resident((2 * FN_R, FN_R))],
        out_specs=[blk, blk],
        out_shape=[t_shape, t_shape],
        compiler_params=pltpu.CompilerParams(vmem_limit_bytes=VMEM_LIMIT),
        name="fnet_stage1",
    )(x4, modv, g.reshape(1, D), l1)


def _fnet_s2_kernel(tr_ref, ti_ref, wc_ref, g_ref, out_ref):
    kb = pl.program_id(1)
    tr = tr_ref[...].reshape(FN_J * FN_R, D)
    ti = ti_ref[...].reshape(FN_J * FN_R, D)
    ur, ui = [], []
    for gi in range(FN_G):
        ls = slice(gi * FN_C, (gi + 1) * FN_C)
        u = _dot(jnp.concatenate([tr[:, ls], ti[:, ls]], axis=1), wc_ref[...])
        ur.append(u[:, :FN_C].astype(bf16))
        ui.append(u[:, FN_C:].astype(bf16))
    ur = jnp.concatenate(ur, axis=1)
    ui = jnp.concatenate(ui, axis=1)
    ys = []
    for j in range(FN_J):
        rows = slice(j * FN_R, (j + 1) * FN_R)
        t = jnp.concatenate([ur[rows], ui[rows]], axis=0)
        ys.append(_dot(g_ref[kb * FN_J + j], t))
    y = _regroup(jnp.concatenate(ys, axis=0), FN_J, FN_R)
    out_ref[...] = (y * (1.0 / math.sqrt(S * FN_C))).astype(bf16).reshape(FN_R, FN_J, D)


def _fnet_s2_call(tr, ti, wc, gtab):
    t_blk = pl.BlockSpec((None, FN_J, FN_R, D), lambda b, kb: (b, kb, 0, 0))
    return pl.pallas_call(
        _fnet_s2_kernel,
        grid=(B, FN_R // FN_J),
        in_specs=[t_blk, t_blk, _resident((2 * FN_C, 2 * FN_C)), _resident((FN_R, FN_R, 2 * FN_R))],
        out_specs=pl.BlockSpec((None, FN_R, FN_J, D), lambda b, kb: (b, 0, kb, 0)),
        out_shape=jax.ShapeDtypeStruct((B, FN_R, FN_R, D), bf16),
        compiler_params=pltpu.CompilerParams(vmem_limit_bytes=VMEM_LIMIT),
        name="fnet_stage2",
    )(tr, ti, wc, gtab)


def _cos_sin(num, den):
    ang = (np.asarray(num) % den).astype(np.float64) * (2.0 * np.pi / den)
    return np.cos(ang), np.sin(ang)


def _fnet_tables():
    ch = np.arange(FN_C)
    cc, sc = _cos_sin(ch[:, None] * ch[None, :], FN_C)
    r = np.arange(FN_R)
    c1, s1 = _cos_sin(r[:, None] * r[None, :], FN_R)
    l1 = np.concatenate([c1, -s1], axis=0)
    wc = np.block([[cc, -sc], [sc, cc]])
    k = r[:, None, None] + FN_R * r[None, :, None]
    gc, gs = _cos_sin(k * r[None, None, :], S)
    gtab = np.concatenate([gc, gs], axis=-1)
    return tuple(jnp.asarray(t, dtype=f32).astype(bf16) for t in (l1, wc, gtab))


def kernel(x, c, ctx, c_ctx, mod_w, mod_b, mix_norm_g, ffn_norm_g, conv_w_in, conv_w, conv_w_out,
           nat_w_qkv, nat_rpb, nat_w_o, mla_w_dq, mla_q_norm_g, mla_w_uq, mla_w_dkv, mla_kv_norm_g,
           mla_w_ukv, mla_w_o, fnet_w_o, ffn_w_in, ffn_w_out, final_norm_g):
    cc8 = jnp.concatenate([c, c_ctx[None, :], jnp.zeros((8 - B - 1, D), f32)], axis=0)
    modv = _mod_call(cc8, mod_w, mod_b).reshape(DEPTH * 8, 1, N_MOD * D)

    w_in = ffn_w_in.astype(bf16)
    w_out = ffn_w_out.astype(bf16)
    x_lat = x.reshape(T_LAT, D)
    x_ctx = ctx.reshape(T_CTX, D)

    cw8 = jnp.concatenate([conv_w[0], jnp.zeros((5, D), f32)], axis=0)
    bz = _conv_call(x_lat, x_ctx, modv, 0, mix_norm_g[0], conv_w_in[0].astype(bf16), cw8)
    xs = _tail_call((x_lat, x_ctx), bz, modv, 0, conv_w_out[0].astype(bf16), ffn_norm_g[0], w_in, w_out,
                    NT)

    qkv = _qkv_call(xs, modv, 1, mix_norm_g[1], nat_w_qkv[0].astype(bf16))
    o = _nat_call(qkv, _nat_bias_table(nat_rpb[0]))
    xs = _tail_call(xs, o, modv, 1, nat_w_o[0].astype(bf16), ffn_norm_g[1], w_in, w_out, NT)

    w1, wqa, wqb, wk, wv = _mla_weights(mla_w_dq[0], mla_w_uq[0], mla_w_dkv[0], mla_w_ukv[0])
    cos_t, sin_t = _rope_tables()
    q, k, v = _mla_proj_call(xs, modv, 2, mix_norm_g[2], w1, mla_q_norm_g[0], mla_kv_norm_g[0],
                             wqa, wqb, wk, wv, cos_t, sin_t)
    o = _mla_attn_call(q, k, v)
    xl = _tail_call(xs, o, modv, 2, mla_w_o[0].astype(bf16), ffn_norm_g[2], w_in, w_out, T_LAT)

    l1_t, wc_t, g_t = _fnet_tables()
    tr, ti = _fnet_s1_call(xl.reshape(B, FN_R, FN_R, D), modv, 3, mix_norm_g[3], l1_t)
    o = _fnet_s2_call(tr, ti, wc_t, g_t).reshape(T_LAT, D)
    out = _tail_call(xl, o, modv, 3, fnet_w_o[0].astype(bf16), ffn_norm_g[3], w_in, w_out,
                     T_LAT, final_g=final_norm_g)
    return out.reshape(B, S, D)
```

```python
import functools
import math

import numpy as np
import jax
import jax.numpy as jnp
from jax import lax
from jax.experimental import pallas as pl
from jax.experimental.pallas import tpu as pltpu

D = 1024
B = 4
S = 4096
CTX = 256
DEPTH = 4
GRID_W = 64
N_MOD = 6
F = 2816
EPS = 1e-6

T_LAT = B * S
T_CTX = B * CTX
NT = T_LAT + T_CTX

NA_H = 16
NA_KR = 8
NA_KC = 16
ML_H = 16
ML_RANK = 256
ML_NOPE = 64
ML_ROPE = 32
ML_V = 64
HP = 8
LANE = 128
MASK_NEG = -1e30

TM = 512
TAIL_SUB = 2
TMT = TAIL_SUB * TM
FFN_CK = 256
VMEM_LIMIT = 56 * 1024 * 1024

f32 = jnp.float32
bf16 = jnp.bfloat16


def _resident(shape):
    nd = len(shape)
    return pl.BlockSpec(shape, lambda *_: (0,) * nd, pipeline_mode=pl.Buffered(1))


def _mod_spec_steps(layer):
    return pl.BlockSpec((None, 1, N_MOD * D), lambda t: (layer * 8 + t // (S // TMT), 0, 0))


def _rmsnorm(x, g):
    ms = jnp.mean(x * x, axis=-1, keepdims=True)
    return (x * lax.rsqrt(ms + EPS)) * g


def _norm_mod(x, g, shift, scale):
    return _rmsnorm(x, g) * (1.0 + scale) + shift


def _dot(a, b):
    return jnp.dot(a, b, preferred_element_type=f32)


def _dot_nt(a, b):
    return lax.dot_general(a, b, (((1,), (1,)), ((), ())), preferred_element_type=f32)


def _mod_kernel(cc_ref, w_ref, b_ref, out_ref):
    cc = cc_ref[...]
    s = (cc * jax.nn.sigmoid(cc)).astype(bf16)
    out_ref[...] = _dot(s, w_ref[...].astype(bf16)) + b_ref[...]


def _mod_call(cc8, mod_w, mod_b):
    tn = 1536
    return pl.pallas_call(
        _mod_kernel,
        grid=(DEPTH, N_MOD * D // tn),
        in_specs=[
            pl.BlockSpec((8, D), lambda l, j: (0, 0)),
            pl.BlockSpec((None, D, tn), lambda l, j: (l, 0, j)),
            pl.BlockSpec((None, 1, tn), lambda l, j: (l, 0, j)),
        ],
        out_specs=pl.BlockSpec((None, 8, tn), lambda l, j: (l, 0, j)),
        out_shape=jax.ShapeDtypeStruct((DEPTH, 8, N_MOD * D), f32),
        compiler_params=pltpu.CompilerParams(vmem_limit_bytes=VMEM_LIMIT),
        name="mod_vectors",
    )(cc8, mod_w, mod_b.reshape(DEPTH, 1, N_MOD * D))


def _tile_rows(lat_ref, ctx_ref, n_lat_steps):
    return jnp.where(pl.program_id(0) < n_lat_steps, lat_ref[...], ctx_ref[...])


def _split_specs(block_rows, lat_idx, ctx_idx, n_lat_steps):
    tl = lambda t: jnp.minimum(t, n_lat_steps - 1)
    tc = lambda t: jnp.maximum(t - n_lat_steps, 0)
    return [pl.BlockSpec((block_rows, D), lambda t: (lat_idx(tl(t)), 0)),
            pl.BlockSpec((block_rows, D), lambda t: (ctx_idx(tc(t)), 0))]


def _tail_kernel(x_ref, o_ref, mod_ref, wo_ref, g_ref, win_ref, wout_ref, *rest, final):
    x = x_ref[...]
    if final:
        fg_ref, out_ref, acc_ref = rest
    else:
        out_ref, acc_ref = rest
    mod = mod_ref[...]
    g1 = mod[:, 2 * D:3 * D]
    sh2 = mod[:, 3 * D:4 * D]
    sc2 = mod[:, 4 * D:5 * D]
    g2 = mod[:, 5 * D:6 * D]
    tiles = [slice(u * TM, (u + 1) * TM) for u in range(TAIL_SUB)]
    x1 = [x[rows] + g1 * _dot(o_ref[rows, :], wo_ref[...]) for rows in tiles]
    h = [_norm_mod(v, g_ref[...], sh2, sc2).astype(bf16) for v in x1]
    for u in range(TAIL_SUB):
        for c in range(F // FFN_CK):
            lo = c * FFN_CK
            gate = _dot(h[u], win_ref[:, lo:lo + FFN_CK])
            up = _dot(h[u], win_ref[:, F + lo:F + lo + FFN_CK])
            a = (gate * jax.nn.sigmoid(gate) * up).astype(bf16)
            y = _dot(a, wout_ref[lo:lo + FFN_CK, :])
            if c == 0:
                acc_ref[u] = y
            else:
                acc_ref[u] += y
    for u, rows in enumerate(tiles):
        x2 = x1[u] + g2 * acc_ref[u]
        if final:
            x2 = _rmsnorm(x2, fg_ref[...])
        out_ref[rows, :] = x2


def _tail_call(x, o, modv, layer, wo, g, win, wout, n_rows, final_g=None):
    final = final_g is not None
    n_steps = n_rows // TMT
    row = lambda w: pl.BlockSpec((TMT, w), lambda t: (t, 0))
    slab = lambda r, c: pl.BlockSpec((None, r, c), lambda t: (layer, 0, 0), pipeline_mode=pl.Buffered(1))
    in_specs = [row(D), row(D), _mod_spec_steps(layer), _resident((D, D)), _resident((1, D)),
                slab(D, 2 * F), slab(F, D)]
    args = [x, o, modv, wo, g.reshape(1, D), win, wout]
    if final:
        in_specs.append(_resident((1, D)))
        args.append(final_g.reshape(1, D))
    return pl.pallas_call(
        functools.partial(_tail_kernel, final=final),
        grid=(n_steps,),
        in_specs=in_specs,
        out_specs=row(D),
        out_shape=jax.ShapeDtypeStruct((n_rows, D), f32),
        scratch_shapes=[pltpu.VMEM((TAIL_SUB, TM, D), f32)],
        compiler_params=pltpu.CompilerParams(vmem_limit_bytes=VMEM_LIMIT),
        name=f"tail_l{layer}",
    )(*args)


CONV_HALO = 16
CONV_CK = 256


def _conv_kernel(lp_ref, cp_ref, lx_ref, cx_ref, ln_ref, cn_ref, mod_ref, g_ref, win_ref, cw_ref,
                 out_ref, xs_ref):
    t = pl.program_id(0)
    n_lat = T_LAT // TMT
    mod = mod_ref[...]
    sh1 = mod[:, 0:D]
    sc1 = mod[:, D:2 * D]
    g = g_ref[...]
    pieces = [_tile_rows(a, b, n_lat) for a, b in ((lp_ref, cp_ref), (lx_ref, cx_ref), (ln_ref, cn_ref))]
    xs_ref[...] = pieces[1]
    h_all = jnp.concatenate([_norm_mod(p, g, sh1, sc1).astype(bf16) for p in pieces], axis=0)
    rows = TM + 2 * CONV_HALO
    seq_len = jnp.where(t < n_lat, S, CTX)
    for u in range(TAIL_SUB):
        h = h_all[u * TM:u * TM + rows]
        grow = t * TMT + u * TM - CONV_HALO + lax.broadcasted_iota(jnp.int32, (rows, CONV_CK), 0)
        pos = jnp.bitwise_and(grow, seq_len - 1)
        has_prev = pos != 0
        has_next = pos != seq_len - 1
        for j in range(D // CONV_CK):
            lo = j * CONV_CK
            bg = _dot(h, win_ref[:, lo:lo + CONV_CK])
            cg = _dot(h, win_ref[:, D + lo:D + lo + CONV_CK])
            v = _dot(h, win_ref[:, 2 * D + lo:2 * D + lo + CONV_CK])
            uu = cg * v
            u_prev = jnp.where(has_prev, pltpu.roll(uu, 1, axis=0), 0.0)
            u_next = jnp.where(has_next, pltpu.roll(uu, rows - 1, axis=0), 0.0)
            z = (cw_ref[0:1, lo:lo + CONV_CK] * u_prev + cw_ref[1:2, lo:lo + CONV_CK] * uu
                 + cw_ref[2:3, lo:lo + CONV_CK] * u_next)
            bz = bg * z
            out_ref[u * TM:(u + 1) * TM, lo:lo + CONV_CK] = bz[CONV_HALO:CONV_HALO + TM].astype(bf16)


def _conv_call(x_lat, x_ctx, modv, layer, g, win, cw):
    hb = TMT // CONV_HALO
    n_lat = T_LAT // TMT
    prev_idx = lambda i: jnp.maximum(i * hb - 1, 0)
    next_idx = lambda rows: (lambda i: jnp.minimum((i + 1) * hb, rows // CONV_HALO - 1))
    prev_l, prev_c = _split_specs(CONV_HALO, prev_idx, prev_idx, n_lat)
    next_l, next_c = _split_specs(CONV_HALO, next_idx(T_LAT), next_idx(T_CTX), n_lat)
    main_l, main_c = _split_specs(TMT, lambda i: i, lambda i: i, n_lat)
    return pl.pallas_call(
        _conv_kernel,
        grid=(NT // TMT,),
        in_specs=[prev_l, prev_c, main_l, main_c, next_l, next_c,
                  _mod_spec_steps(layer), _resident((1, D)), _resident((D, 3 * D)), _resident((8, D))],
        out_specs=[pl.BlockSpec((TMT, D), lambda t: (t, 0))] * 2,
        out_shape=[jax.ShapeDtypeStruct((NT, D), bf16), jax.ShapeDtypeStruct((NT, D), f32)],
        compiler_params=pltpu.CompilerParams(vmem_limit_bytes=VMEM_LIMIT),
        name="conv_mixer",
    )(x_lat, x_ctx, x_lat, x_ctx, x_lat, x_ctx, modv, g.reshape(1, D), win, cw)


QKV_CK = 512
NA_SCALE = 0.125


def _qkv_kernel(x_ref, mod_ref, g_ref, w_ref, out_ref):
    mod = mod_ref[...]
    tiles = [slice(u * TM, (u + 1) * TM) for u in range(TAIL_SUB)]
    hs = [_norm_mod(x_ref[rows, :], g_ref[...], mod[:, 0:D], mod[:, D:2 * D]).astype(bf16) for rows in tiles]
    for h, rows in zip(hs, tiles):
        for c in range(3 * D // QKV_CK):
            lo = c * QKV_CK
            r = _dot(h, w_ref[:, lo:lo + QKV_CK])
            if lo < D:
                r = r * NA_SCALE
            out_ref[rows, lo:lo + QKV_CK] = r.astype(bf16)


def _qkv_call(x, modv, layer, g, w):
    return pl.pallas_call(
        _qkv_kernel,
        grid=(NT // TMT,),
        in_specs=[pl.BlockSpec((TMT, D), lambda t: (t, 0)), _mod_spec_steps(layer),
                  _resident((1, D)), _resident((D, 3 * D))],
        out_specs=pl.BlockSpec((TMT, 3 * D), lambda t: (t, 0)),
        out_shape=jax.ShapeDtypeStruct((NT, 3 * D), bf16),
        compiler_params=pltpu.CompilerParams(vmem_limit_bytes=VMEM_LIMIT),
        name="nat_qkv",
    )(x, modv, g.reshape(1, D), w)


def _stack_pair(q2):
    lane = lax.broadcasted_iota(jnp.int32, q2.shape, 1)
    zero = jnp.zeros_like(q2)
    return jnp.concatenate([jnp.where(lane < 64, q2, zero), jnp.where(lane >= 64, q2, zero)], axis=0)


def _unstack_pair(o, n):
    lane = lax.broadcasted_iota(jnp.int32, (n, LANE), 1)
    return jnp.where(lane < 64, o[:n], o[n:])


NA_ROWS = S // GRID_W
NA_SUB = 4
NA_WIN_STEPS = NA_ROWS // NA_SUB
NA_CTX_STEPS = CTX // (NA_SUB * GRID_W)


def _nat_kernel(q_ref, kb_ref, vb_ref, kc_ref, vc_ref, bias_ref, out_ref):
    step = pl.program_id(1)
    lanes = [slice(hp * LANE, (hp + 1) * LANE) for hp in range(HP)]
    nband = NA_KR * GRID_W

    def attend(with_window):
        scores = []
        for sub in range(NA_SUB):
            rows = slice(sub * GRID_W, (sub + 1) * GRID_W)
            r = step * NA_SUB + sub
            r0 = jnp.clip(r - NA_KR // 2, 0, NA_ROWS - NA_KR)
            off = pl.multiple_of(r0 * GRID_W, GRID_W)
            for hp, ls in enumerate(lanes):
                qst = _stack_pair(q_ref[rows, ls])
                s_c = _dot_nt(qst, kc_ref[:, ls])
                s_w = None
                if with_window:
                    s_w = _dot_nt(qst, kb_ref[pl.ds(off, nband), ls]) + bias_ref[r0 - r + NA_KR - 1, hp]
                scores.append((s_w, s_c, off, rows, ls))
        probs = []
        for s_w, s_c, off, rows, ls in scores:
            m = jnp.max(s_c, axis=-1, keepdims=True)
            if with_window:
                m = jnp.maximum(m, jnp.max(s_w, axis=-1, keepdims=True))
            p_c = jnp.exp(s_c - m)
            l = jnp.sum(p_c, axis=-1, keepdims=True)
            p_w = None
            if with_window:
                p_w = jnp.exp(s_w - m)
                l = l + jnp.sum(p_w, axis=-1, keepdims=True)
                p_w = p_w.astype(bf16)
            probs.append((p_w, p_c.astype(bf16), l, off, rows, ls))
        for p_w, p_c, l, off, rows, ls in probs:
            o = _dot(p_c, vc_ref[:, ls])
            if with_window:
                o = o + _dot(p_w, vb_ref[pl.ds(off, nband), ls])
            out_ref[rows, ls] = _unstack_pair(o / l, GRID_W).astype(bf16)

    @pl.when(step < NA_WIN_STEPS)
    def _():
        attend(True)

    @pl.when(step >= NA_WIN_STEPS)
    def _():
        attend(False)


def _nat_call(qkv, bias):
    tq = NA_SUB * GRID_W

    def q_idx(b, s):
        ctx_blk = T_LAT // tq + b * NA_CTX_STEPS + (s - NA_WIN_STEPS)
        return (jnp.where(s < NA_WIN_STEPS, b * NA_WIN_STEPS + s, ctx_blk), 0)

    once = dict(pipeline_mode=pl.Buffered(1))
    ctx_blk = T_LAT // CTX
    return pl.pallas_call(
        _nat_kernel,
        grid=(B, NA_WIN_STEPS + NA_CTX_STEPS),
        in_specs=[
            pl.BlockSpec((tq, D), q_idx),
            pl.BlockSpec((S, D), lambda b, s: (b, 1), **once),
            pl.BlockSpec((S, D), lambda b, s: (b, 2), **once),
            pl.BlockSpec((CTX, D), lambda b, s: (ctx_blk + b, 1)),
            pl.BlockSpec((CTX, D), lambda b, s: (ctx_blk + b, 2)),
            _resident((NA_KR, HP, 2 * GRID_W, NA_KR * GRID_W)),
        ],
        out_specs=pl.BlockSpec((tq, D), q_idx),
        out_shape=jax.ShapeDtypeStruct((NT, D), bf16),
        compiler_params=pltpu.CompilerParams(vmem_limit_bytes=VMEM_LIMIT),
        name="nat_attn",
    )(qkv, qkv, qkv, qkv, qkv, bias)


def _nat_bias_kernel(rpb_ref, out_ref):
    d0 = pl.program_id(0)
    q = lax.broadcasted_iota(jnp.int32, (GRID_W, LANE), 0)
    lane = lax.broadcasted_iota(jnp.int32, (GRID_W, LANE), 1)
    j = lane % GRID_W
    c0 = jnp.clip(q - NA_KC // 2, 0, GRID_W - NA_KC)
    valid = (j >= c0) & (j < c0 + NA_KC)
    for h in range(NA_H):
        hp, e = divmod(h, 2)
        for m in range(NA_KR // 2):
            halves = []
            for u in range(2):
                row = jnp.broadcast_to(rpb_ref[h, pl.ds(d0 + 2 * m + u, 1), :], (GRID_W, LANE))
                shift = (LANE - (NA_KC - 1) + u * GRID_W) % LANE
                halves.append(pltpu.roll(row, shift, axis=1, stride=1, stride_axis=0))
            t = jnp.where(lane < GRID_W, halves[0], halves[1])
            out_ref[hp, e * GRID_W:(e + 1) * GRID_W, m * LANE:(m + 1) * LANE] = jnp.where(valid, t, MASK_NEG)


def _nat_bias_table(rpb):
    rows, taps = 2 * NA_KR - 1, 2 * NA_KC - 1
    rp = jnp.pad(rpb, ((0, 0), (0, 2 * NA_KR - rows), (0, LANE - taps)))
    return pl.pallas_call(
        _nat_bias_kernel,
        grid=(NA_KR,),
        in_specs=[pl.BlockSpec((NA_H, 2 * NA_KR, LANE), lambda d: (0, 0, 0))],
        out_specs=pl.BlockSpec((None, HP, 2 * GRID_W, NA_KR * GRID_W), lambda d: (d, 0, 0, 0)),
        out_shape=jax.ShapeDtypeStruct((NA_KR, HP, 2 * GRID_W, NA_KR * GRID_W), f32),
        compiler_params=pltpu.CompilerParams(vmem_limit_bytes=VMEM_LIMIT),
        name="nat_bias",
    )(rp)


ML_SCALE = (ML_NOPE + ML_ROPE) ** -0.5 * math.log2(math.e)
ML_QW = ML_H * LANE
ML_P1 = 2 * ML_RANK + 2 * LANE
ML_CK = 256


def _mla_proj_kernel(x_ref, mod_ref, g_ref, w1_ref, qg_ref, kvg_ref, wqa_ref, wqb_ref, wk_ref, wv_ref,
                     cos_ref, sin_ref, q_out, k_out, v_out):
    mod = mod_ref[...]
    tiles = [slice(u * TM, (u + 1) * TM) for u in range(TAIL_SUB)]
    lat = []
    for rows in tiles:
        h = _norm_mod(x_ref[rows, :], g_ref[...], mod[:, 0:D], mod[:, D:2 * D]).astype(bf16)
        p = _dot(h, w1_ref[...])
        cq = _rmsnorm(p[:, 0:ML_RANK], qg_ref[...]).astype(bf16)
        ckv = _rmsnorm(p[:, ML_RANK:2 * ML_RANK], kvg_ref[...]).astype(bf16)
        cos = cos_ref[rows, :]
        sin = sin_ref[rows, :]
        krope = p[:, 2 * ML_RANK:2 * ML_RANK + LANE] * cos + p[:, 2 * ML_RANK + LANE:] * sin
        lat.append((cq, ckv, cos, sin, krope))
    ones_lane = lax.broadcasted_iota(jnp.int32, (TM, ML_CK), 1) % LANE >= ML_V
    for (cq, ckv, cos, sin, krope), rows in zip(lat, tiles):
        for c in range(ML_QW // ML_CK):
            lo = c * ML_CK
            qa = _dot(cq, wqa_ref[:, lo:lo + ML_CK])
            qb = _dot(cq, wqb_ref[:, lo:lo + ML_CK])
            kn = _dot(ckv, wk_ref[:, lo:lo + ML_CK])
            for e in range(ML_CK // LANE):
                ls = slice(e * LANE, (e + 1) * LANE)
                q_out[rows, lo + e * LANE:lo + (e + 1) * LANE] = (
                    (qa[:, ls] * cos + qb[:, ls] * sin) * ML_SCALE).astype(bf16)
                k_out[rows, lo + e * LANE:lo + (e + 1) * LANE] = (kn[:, ls] + krope).astype(bf16)
        for c in range(ML_QW // ML_CK):
            lo = c * ML_CK
            vv = _dot(ckv, wv_ref[:, lo:lo + ML_CK])
            v_out[rows, lo:lo + ML_CK] = jnp.where(ones_lane, 1.0, vv).astype(bf16)


def _mla_proj_call(x, modv, layer, g, w1, qg, kvg, wqa, wqb, wk, wv, cos_t, sin_t):
    lat_steps, steps_per_seq = T_LAT // TMT, S // TMT
    rope_idx = lambda t: (jnp.where(t < lat_steps, t % steps_per_seq, steps_per_seq), 0)
    row = lambda w: pl.BlockSpec((TMT, w), lambda t: (t, 0))
    return pl.pallas_call(
        _mla_proj_kernel,
        grid=(NT // TMT,),
        in_specs=[row(D), _mod_spec_steps(layer), _resident((1, D)), _resident((D, ML_P1)),
                  _resident((1, ML_RANK)), _resident((1, ML_RANK)),
                  _resident((ML_RANK, ML_QW)), _resident((ML_RANK, ML_QW)),
                  _resident((ML_RANK, ML_QW)), _resident((ML_RANK, ML_QW)),
                  pl.BlockSpec((TMT, LANE), rope_idx), pl.BlockSpec((TMT, LANE), rope_idx)],
        out_specs=[row(ML_QW), row(ML_QW), row(ML_QW)],
        out_shape=[jax.ShapeDtypeStruct((NT, ML_QW), bf16)] * 3,
        compiler_params=pltpu.CompilerParams(vmem_limit_bytes=VMEM_LIMIT),
        name="mla_proj",
    )(x, modv, g.reshape(1, D), w1, qg.reshape(1, ML_RANK), kvg.reshape(1, ML_RANK),
      wqa, wqb, wk, wv, cos_t, sin_t)


ML_TQ = 1024
ML_EDGES = (0, 2048, S)


def _mla_attn_kernel(q_ref, kl_ref, kc_ref, vl_ref, vc_ref, out_ref):
    qs = [q_ref[:, e * LANE:(e + 1) * LANE] for e in range(2)]
    n_lat = len(ML_EDGES) - 1

    def chunk(lat_ref, ctx_ref, c):
        lat = lat_ref[ML_EDGES[c]:ML_EDGES[c + 1], :]
        return lat if c < n_lat - 1 else jnp.concatenate([lat, ctx_ref[...]], axis=0)

    def keys(c):
        return chunk(kl_ref, kc_ref, c)

    def values(c):
        return chunk(vl_ref, vc_ref, c)

    def update(e, c, carry):
        ls = slice(e * LANE, (e + 1) * LANE)
        s = _dot_nt(qs[e], keys(c)[:, ls])
        m_new = jnp.max(s, axis=-1, keepdims=True)
        if carry is None:
            return m_new, _dot(jnp.exp2(s - m_new).astype(bf16), values(c)[:, ls])
        m, acc = carry
        m_new = jnp.maximum(m, m_new)
        alpha = jnp.exp2(m - m_new)
        p = jnp.exp2(s - m_new)
        acc = alpha * acc + _dot(p.astype(bf16), values(c)[:, ls])
        return m_new, acc

    carry = [None, None]
    for c in range(n_lat):
        carry = [update(e, c, carry[e]) for e in range(2)]
    a0, a1 = carry[0][1], carry[1][1]
    lane = lax.broadcasted_iota(jnp.int32, (ML_TQ, LANE), 1)
    num = jnp.where(lane < ML_V, a0, pltpu.roll(a1, ML_V, axis=1))
    den = jnp.where(lane < ML_V, pltpu.roll(a0, ML_V, axis=1), a1)
    out_ref[...] = (num / den).astype(bf16)


def _mla_attn_call(q, k, v):
    nq = S // ML_TQ
    ctx_blk = T_LAT // CTX
    return pl.pallas_call(
        _mla_attn_kernel,
        grid=(B, HP, nq),
        in_specs=[
            pl.BlockSpec((ML_TQ, 2 * LANE), lambda b, hp, i: (b * nq + i, hp)),
            pl.BlockSpec((S, 2 * LANE), lambda b, hp, i: (b, hp)),
            pl.BlockSpec((CTX, 2 * LANE), lambda b, hp, i: (ctx_blk + b, hp)),
            pl.BlockSpec((S, 2 * LANE), lambda b, hp, i: (b, hp)),
            pl.BlockSpec((CTX, 2 * LANE), lambda b, hp, i: (ctx_blk + b, hp)),
        ],
        out_specs=pl.BlockSpec((ML_TQ, LANE), lambda b, hp, i: (b * nq + i, hp)),
        out_shape=jax.ShapeDtypeStruct((T_LAT, D), bf16),
        compiler_params=pltpu.CompilerParams(vmem_limit_bytes=VMEM_LIMIT),
        name="mla_attn",
    )(q, k, k, v, v)


def _mla_weights(w_dq, w_uq, w_dkv, w_ukv):
    z = lambda *s: jnp.zeros(s, f32)
    uq = w_uq.reshape(ML_RANK, ML_H, ML_NOPE + ML_ROPE)
    nope, r1, r2 = uq[..., :ML_NOPE], uq[..., ML_NOPE:ML_NOPE + 16], uq[..., ML_NOPE + 16:]
    pad = z(ML_RANK, ML_H, LANE - ML_NOPE - ML_ROPE)
    wqa = jnp.concatenate([nope, r1, r2, pad], axis=-1).reshape(ML_RANK, ML_QW)
    wqb = jnp.concatenate([jnp.zeros_like(nope), -r2, r1, pad], axis=-1).reshape(ML_RANK, ML_QW)
    ukv = w_ukv.reshape(ML_RANK, ML_H, ML_NOPE + ML_V)
    wk = jnp.concatenate([ukv[..., :ML_NOPE], z(ML_RANK, ML_H, LANE - ML_NOPE)], axis=-1).reshape(ML_RANK, ML_QW)
    wv = jnp.concatenate([ukv[..., ML_NOPE:], z(ML_RANK, ML_H, LANE - ML_V)], axis=-1).reshape(ML_RANK, ML_QW)
    wr = w_dkv[:, ML_RANK:]
    rope1 = jnp.concatenate([z(D, ML_NOPE), wr, z(D, LANE - ML_NOPE - ML_ROPE)], axis=-1)
    rope2 = jnp.concatenate([z(D, ML_NOPE), -wr[:, 16:], wr[:, :16], z(D, LANE - ML_NOPE - ML_ROPE)], axis=-1)
    w1 = jnp.concatenate([w_dq, w_dkv[:, :ML_RANK], rope1, rope2], axis=-1)
    return tuple(w.astype(bf16) for w in (w1, wqa, wqb, wk, wv))


def _rope_tables():
    n_freq = ML_ROPE // 4
    freq = 10000.0 ** (-jnp.arange(n_freq, dtype=f32) / n_freq)
    t = jnp.arange(S)
    row = (t // GRID_W).astype(f32)
    col = (t % GRID_W).astype(f32)
    ang = jnp.concatenate([row[:, None] * freq, col[:, None] * freq], axis=-1)
    cos, sin = jnp.cos(ang), jnp.sin(ang)
    pad = jnp.zeros((S, LANE - ML_NOPE - ML_ROPE), f32)
    cos_l = jnp.concatenate([jnp.ones((S, ML_NOPE), f32), cos, cos, pad], axis=-1)
    sin_l = jnp.concatenate([jnp.zeros((S, ML_NOPE), f32), sin, sin, pad], axis=-1)
    cos_c = jnp.concatenate([jnp.ones((TMT, ML_NOPE + ML_ROPE), f32), pad[:TMT]], axis=-1)
    return jnp.concatenate([cos_l, cos_c], axis=0), jnp.concatenate([sin_l, jnp.zeros((TMT, LANE), f32)], axis=0)


FN_G = 4
FN_C = D // FN_G
FN_R = 64
FN_J = 16
assert FN_R * FN_R == S


def _regroup(x, outer, inner):
    return x.reshape(outer, inner, x.shape[-1]).swapaxes(0, 1).reshape(x.shape)


def _fnet_s1_kernel(x_ref, mod_ref, g_ref, l1_ref, tr_out, ti_out):
    mod = mod_ref[...]
    x = x_ref[...].reshape(FN_R * FN_J, D)
    h = _norm_mod(x, g_ref[...], mod[:, 0:D], mod[:, D:2 * D])
    h = _regroup(h, FN_R, FN_J).astype(bf16)
    t = jnp.concatenate([_dot(l1_ref[...], h[j * FN_R:(j + 1) * FN_R]) for j in range(FN_J)], axis=0)
    t = _regroup(t, FN_J, 2 * FN_R)
    half = FN_R * FN_J
    tr_out[...] = t[:half].astype(bf16).reshape(FN_R, FN_J, D)
    ti_out[...] = t[half:].astype(bf16).reshape(FN_R, FN_J, D)


def _fnet_s1_call(x4, modv, layer, g, l1):
    blk = pl.BlockSpec((None, FN_R, FN_J, D), lambda b, jb: (b, 0, jb, 0))
    t_shape = jax.ShapeDtypeStruct((B, FN_R, FN_R, D), bf16)
    return pl.pallas_call(
        _fnet_s1_kernel,
        grid=(B, FN_R // FN_J),
        in_specs=[blk, pl.BlockSpec((None, 1, N_MOD * D), lambda b, jb: (layer * 8 + b, 0, 0)),
                  _resident((1, D)), _resident((2 * FN_R, FN_R))],
        out_specs=[blk, blk],
        out_shape=[t_shape, t_shape],
        compiler_params=pltpu.CompilerParams(vmem_limit_bytes=VMEM_LIMIT),
        name="fnet_stage1",
    )(x4, modv, g.reshape(1, D), l1)


def _fnet_s2_kernel(tr_ref, ti_ref, wc_ref, g_ref, out_ref):
    kb = pl.program_id(1)
    tr = tr_ref[...].reshape(FN_J * FN_R, D)
    ti = ti_ref[...].reshape(FN_J * FN_R, D)
    ur, ui = [], []
    for gi in range(FN_G):
        ls = slice(gi * FN_C, (gi + 1) * FN_C)
        u = _dot(jnp.concatenate([tr[:, ls], ti[:, ls]], axis=1), wc_ref[...])
        ur.append(u[:, :FN_C].astype(bf16))
        ui.append(u[:, FN_C:].astype(bf16))
    ur = jnp.concatenate(ur, axis=1)
    ui = jnp.concatenate(ui, axis=1)
    ys = []
    for j in range(FN_J):
        rows = slice(j * FN_R, (j + 1) * FN_R)
        t = jnp.concatenate([ur[rows], ui[rows]], axis=0)
        ys.append(_dot(g_ref[kb * FN_J + j], t))
    y = _regroup(jnp.concatenate(ys, axis=0), FN_J, FN_R)
    out_ref[...] = (y * (1.0 / math.sqrt(S * FN_C))).astype(bf16).reshape(FN_R, FN_J, D)


def _fnet_s2_call(tr, ti, wc, gtab):
    t_blk = pl.BlockSpec((None, FN_J, FN_R, D), lambda b, kb: (b, kb, 0, 0))
    return pl.pallas_call(
        _fnet_s2_kernel,
        grid=(B, FN_R // FN_J),
        in_specs=[t_blk, t_blk, _resident((2 * FN_C, 2 * FN_C)), _resident((FN_R, FN_R, 2 * FN_R))],
        out_specs=pl.BlockSpec((None, FN_R, FN_J, D), lambda b, kb: (b, 0, kb, 0)),
        out_shape=jax.ShapeDtypeStruct((B, FN_R, FN_R, D), bf16),
        compiler_params=pltpu.CompilerParams(vmem_limit_bytes=VMEM_LIMIT),
        name="fnet_stage2",
    )(tr, ti, wc, gtab)


def _cos_sin(num, den):
    ang = (np.asarray(num) % den).astype(np.float64) * (2.0 * np.pi / den)
    return np.cos(ang), np.sin(ang)


def _fnet_tables():
    ch = np.arange(FN_C)
    cc, sc = _cos_sin(ch[:, None] * ch[None, :], FN_C)
    r = np.arange(FN_R)
    c1, s1 = _cos_sin(r[:, None] * r[None, :], FN_R)
    l1 = np.concatenate([c1, -s1], axis=0)
    wc = np.block([[cc, -sc], [sc, cc]])
    k = r[:, None, None] + FN_R * r[None, :, None]
    gc, gs = _cos_sin(k * r[None, None, :], S)
    gtab = np.concatenate([gc, gs], axis=-1)
    return tuple(jnp.asarray(t, dtype=f32).astype(bf16) for t in (l1, wc, gtab))


def kernel(x, c, ctx, c_ctx, mod_w, mod_b, mix_norm_g, ffn_norm_g, conv_w_in, conv_w, conv_w_out,
           nat_w_qkv, nat_rpb, nat_w_o, mla_w_dq, mla_q_norm_g, mla_w_uq, mla_w_dkv, mla_kv_norm_g,
           mla_w_ukv, mla_w_o, fnet_w_o, ffn_w_in, ffn_w_out, final_norm_g):
    cc8 = jnp.concatenate([c, c_ctx[None, :], jnp.zeros((8 - B - 1, D), f32)], axis=0)
    modv = _mod_call(cc8, mod_w, mod_b).reshape(DEPTH * 8, 1, N_MOD * D)

    w_in = ffn_w_in.astype(bf16)
    w_out = ffn_w_out.astype(bf16)
    x_lat = x.reshape(T_LAT, D)
    x_ctx = ctx.reshape(T_CTX, D)

    cw8 = jnp.concatenate([conv_w[0], jnp.zeros((5, D), f32)], axis=0)
    bz, xs = _conv_call(x_lat, x_ctx, modv, 0, mix_norm_g[0], conv_w_in[0].astype(bf16), cw8)
    xs = _tail_call(xs, bz, modv, 0, conv_w_out[0].astype(bf16), ffn_norm_g[0], w_in, w_out, NT)

    qkv = _qkv_call(xs, modv, 1, mix_norm_g[1], nat_w_qkv[0].astype(bf16))
    o = _nat_call(qkv, _nat_bias_table(nat_rpb[0]))
    xs = _tail_call(xs, o, modv, 1, nat_w_o[0].astype(bf16), ffn_norm_g[1], w_in, w_out, NT)

    w1, wqa, wqb, wk, wv = _mla_weights(mla_w_dq[0], mla_w_uq[0], mla_w_dkv[0], mla_w_ukv[0])
    cos_t, sin_t = _rope_tables()
    q, k, v = _mla_proj_call(xs, modv, 2, mix_norm_g[2], w1, mla_q_norm_g[0], mla_kv_norm_g[0],
                             wqa, wqb, wk, wv, cos_t, sin_t)
    o = _mla_attn_call(q, k, v)
    xl = _tail_call(xs, o, modv, 2, mla_w_o[0].astype(bf16), ffn_norm_g[2], w_in, w_out, T_LAT)

    l1_t, wc_t, g_t = _fnet_tables()
    tr, ti = _fnet_s1_call(xl.reshape(B, FN_R, FN_R, D), modv, 3, mix_norm_g[3], l1_t)
    o = _fnet_s2_call(tr, ti, wc_t, g_t).reshape(T_LAT, D)
    out = _tail_call(xl, o, modv, 3, fnet_w_o[0].astype(bf16), ffn_norm_g[3], w_in, w_out,
                     T_LAT, final_g=final_norm_g)
    return out.reshape(B, S, D)
```
